```python
import math
import jax, jax.numpy as jnp
from jax import lax
import numpy as np

D_MODEL = 1024
BATCH = 8
SEQ = 4096
DEPTH = 2

GRID_W = 64
CONV_W = 256
RWKV_HEADS = 4
RWKV_HEAD_DIM = 64
RWKV_W = RWKV_HEADS * RWKV_HEAD_DIM
DECAY_RANK = 16
ICLR_RANK = 16
GATE_RANK = 32
DECAY_SCALE = math.exp(-0.5)
GN_EPS = 64e-5
NA_HEADS = 4
NA_HEAD_DIM = 64
NA_W = NA_HEADS * NA_HEAD_DIM
NA_KH = 8
NA_KW = 16
SGU_HEADS = 4
SGU_HEAD_DIM = 64
SGU_W = SGU_HEADS * SGU_HEAD_DIM
SGU_CHUNK = 128
N_GROUPS = 4
GROUP_W = 256
D_MIX = CONV_W + RWKV_W + NA_W + SGU_W
D_FF = 2816
NORM_EPS = 1e-6
IN_SPLITS = (CONV_W, CONV_W, CONV_W,
             RWKV_W, RWKV_W, RWKV_W,
             DECAY_RANK, DECAY_RANK, ICLR_RANK, ICLR_RANK,
             GATE_RANK,
             NA_W, NA_W, NA_W,
             SGU_W, SGU_W)
D_IN_PROJ = sum(IN_SPLITS)

kernel_name = "hybrid_parallel_heads_encoder"


def rms_norm(x, g):
    x32 = x.astype(jnp.float32)
    y = x32 * lax.rsqrt(jnp.mean(x32 * x32, axis=-1, keepdims=True) + NORM_EPS)
    return (y * g).astype(x.dtype)


def dwconv3(x, w):
    xp = jnp.pad(x, ((0, 0), (1, 1), (0, 0)))
    return xp[:, :-2] * w[0] + xp[:, 1:-1] * w[1] + xp[:, 2:] * w[2]


def wkv7_scan(r, w, k, v, kk, a, reverse):
    B, T, H, dh = r.shape

    def step(S, inp):
        r_t, w_t, k_t, v_t, kk_t, a_t = inp
        sa = jnp.einsum('bhvk,bhk->bhv', S, -kk_t)
        S = (S * w_t[:, :, None, :]
             + sa[..., None] * (kk_t * a_t)[:, :, None, :]
             + v_t[..., None] * k_t[:, :, None, :])
        return S, jnp.einsum('bhvk,bhk->bhv', S, r_t)

    xs = (jnp.swapaxes(r, 0, 1), jnp.swapaxes(w, 0, 1), jnp.swapaxes(k, 0, 1),
          jnp.swapaxes(v, 0, 1), jnp.swapaxes(kk, 0, 1), jnp.swapaxes(a, 0, 1))
    S0 = jnp.zeros((B, H, dh, dh), jnp.float32)
    _, ys = lax.scan(step, S0, xs, reverse=reverse)
    return jnp.swapaxes(ys, 0, 1)


def rwkv7_mix(r, k, v, dec_f, dec_b, iclr_f, iclr_b, g_dn,
              w0, w_up, a0, a_up, g_up, k_k, k_a, r_k, lnx_w, lnx_b):
    dt = r.dtype
    B, T, _ = r.shape
    f = lambda t: t.astype(jnp.float32)
    heads = lambda t: t.reshape(B, T, RWKV_HEADS, RWKV_HEAD_DIM)
    r, k, v = f(r), f(k), f(v)
    rh, kh, vh = heads(r), heads(k), heads(v)
    kk = heads(k * f(k_k))
    kk = kk * lax.rsqrt(jnp.sum(kk * kk, axis=-1, keepdims=True) + 1e-12)
    g = jax.nn.sigmoid(f(g_dn)) @ f(g_up)
    y = jnp.zeros_like(rh)
    for d, (dn_w, dn_a) in enumerate(((dec_f, iclr_f), (dec_b, iclr_b))):
        log_w = -DECAY_SCALE * jax.nn.sigmoid(f(w0[d]) + jnp.tanh(f(dn_w)) @ f(w_up[d]))
        a = jax.nn.sigmoid(f(a0[d]) + f(dn_a) @ f(a_up[d]))
        k_eff = k * (1.0 + (a - 1.0) * f(k_a))
        y = y + wkv7_scan(rh, heads(jnp.exp(log_w)), heads(k_eff), vh, kk, heads(a),
                          reverse=(d == 1))
    mu = jnp.mean(y, axis=-1, keepdims=True)
    var = jnp.mean(jnp.square(y - mu), axis=-1, keepdims=True)
    y = ((y - mu) * lax.rsqrt(var + GN_EPS)).reshape(B, T, RWKV_W) * f(lnx_w) + f(lnx_b)
    bonus = (jnp.sum(rh * kh * f(r_k), axis=-1, keepdims=True) * vh).reshape(B, T, RWKV_W)
    return ((y + bonus) * g).astype(dt)


def neighborhood_attention(q, k, v, rpb):
    B, T, _ = q.shape
    rows = T // GRID_W
    kh = min(NA_KH, rows)
    grid = lambda t: t.reshape(B, rows, GRID_W, NA_HEADS, NA_HEAD_DIM)
    qg = grid(q) * (NA_HEAD_DIM ** -0.5)
    kg, vg = grid(k), grid(v)
    r = jnp.arange(rows)
    row_start = jnp.clip(r - kh // 2, 0, rows - kh)
    row_idx = row_start[:, None] + jnp.arange(kh)[None, :]
    k_band = kg[:, row_idx]
    v_band = vg[:, row_idx]
    s = jnp.einsum('brqhd,brkwhd->bhrqkw', qg, k_band).astype(jnp.float32)
    c = jnp.arange(GRID_W)
    col_start = jnp.clip(c - NA_KW // 2, 0, GRID_W - NA_KW)
    col_mask = (c[None, :] >= col_start[:, None]) & (c[None, :] < col_start[:, None] + NA_KW)
    dy = row_idx - r[:, None] + (NA_KH - 1)
    dx = jnp.clip(c[None, :] - c[:, None], -(NA_KW - 1), NA_KW - 1) + (NA_KW - 1)
    bias = rpb[:, dy[:, None, :, None], dx[None, :, None, :]]
    s = s + bias[None].astype(jnp.float32)
    s = jnp.where(col_mask[None, None, None, :, None, :], s, -1e30)
    p = jax.nn.softmax(s, axis=(-2, -1))
    o = jnp.einsum('bhrqkw,brkwhd->brqhd', p.astype(v.dtype), v_band)
    return o.reshape(B, T, NA_W)


def layer_norm(x, g):
    x32 = x.astype(jnp.float32)
    mu = jnp.mean(x32, axis=-1, keepdims=True)
    var = jnp.mean(jnp.square(x32 - mu), axis=-1, keepdims=True)
    return ((x32 - mu) * lax.rsqrt(var + NORM_EPS) * g).astype(x.dtype)


def spatial_gating(u, v, norm_w, w_s, b_s):
    B, T, _ = u.shape
    u = jax.nn.gelu(u)
    v = layer_norm(jax.nn.gelu(v), norm_w)
    vc = v.reshape(B, T // SGU_CHUNK, SGU_CHUNK, SGU_HEADS, SGU_HEAD_DIM)
    mixed = jnp.einsum('hpq,bnqhd->bnphd', w_s, vc) + jnp.swapaxes(b_s, 0, 1)[:, :, None]
    return u * mixed.reshape(B, T, SGU_W)


def setup_inputs(seed: int = 0) -> dict:
    key = jax.random.key(seed)
    ks = jax.random.split(key, 26)
    L = DEPTH
    nrm = lambda k, shape, scale: jax.random.normal(k, shape, jnp.float32) * scale
    gain = lambda k, shape, base=1.0: base + 0.05 * jax.random.normal(k, shape, jnp.float32)
    return {
        "x": nrm(ks[0], (BATCH, SEQ, D_MODEL), 1.0),
        "norm_mix_pre": gain(ks[1], (L, D_MODEL)),
        "norm_mix_post": gain(ks[2], (L, D_MODEL)),
        "norm_ffn_pre": gain(ks[3], (L, D_MODEL)),
        "norm_ffn_post": gain(ks[4], (L, D_MODEL)),
        "w_in": nrm(ks[5], (L, D_MODEL, D_IN_PROJ), D_MODEL ** -0.5),
        "conv_a_w": nrm(ks[6], (L, 3, CONV_W), 3 ** -0.5),
        "rwkv_w0": nrm(ks[7], (L, 2, RWKV_W), 0.5),
        "rwkv_w_up": nrm(ks[8], (L, 2, DECAY_RANK, RWKV_W), DECAY_RANK ** -0.5),
        "rwkv_a0": nrm(ks[9], (L, 2, RWKV_W), 0.5),
        "rwkv_a_up": nrm(ks[10], (L, 2, ICLR_RANK, RWKV_W), ICLR_RANK ** -0.5),
        "rwkv_g_up": nrm(ks[11], (L, GATE_RANK, RWKV_W), GATE_RANK ** -0.5),
        "rwkv_k_k": gain(ks[12], (L, RWKV_W), 0.85),
        "rwkv_k_a": gain(ks[13], (L, RWKV_W)),
        "rwkv_r_k": nrm(ks[14], (L, RWKV_HEADS, RWKV_HEAD_DIM), 0.1),
        "rwkv_lnx_w": gain(ks[15], (L, RWKV_W)),
        "rwkv_lnx_b": nrm(ks[16], (L, RWKV_W), 0.02),
        "na_rpb": nrm(ks[17], (L, NA_HEADS, 2 * NA_KH - 1, 2 * NA_KW - 1), 0.1),
        "sgu_norm": gain(ks[18], (L, SGU_W)),
        "sgu_w": nrm(ks[19], (L, SGU_HEADS, SGU_CHUNK, SGU_CHUNK), SGU_CHUNK ** -0.5),
        "sgu_b": gain(ks[20], (L, SGU_HEADS, SGU_CHUNK)),
        "merge_gain": gain(ks[21], (L, D_MIX)),
        "w_out": nrm(ks[22], (L, D_MIX, D_MODEL), D_MIX ** -0.5),
        "ffn_w_up": nrm(ks[23], (L, D_MODEL, 2 * D_FF), D_MODEL ** -0.5),
        "ffn_conv": nrm(ks[24], (L, 3, D_FF), 3 ** -0.5),
        "ffn_w_down": nrm(ks[25], (L, D_FF, D_MODEL), D_FF ** -0.5),
    }


def reference(x, norm_mix_pre, norm_mix_post, norm_ffn_pre, norm_ffn_post, w_in, conv_a_w,
              rwkv_w0, rwkv_w_up, rwkv_a0, rwkv_a_up, rwkv_g_up, rwkv_k_k, rwkv_k_a, rwkv_r_k,
              rwkv_lnx_w, rwkv_lnx_b, na_rpb, sgu_norm, sgu_w, sgu_b, merge_gain, w_out,
              ffn_w_up, ffn_conv, ffn_w_down):
    B, T, _ = x.shape
    split_points = np.cumsum(IN_SPLITS)[:-1].tolist()
    for l in range(DEPTH):
        h = rms_norm(x, norm_mix_pre[l])
        proj = h @ w_in[l]
        (c_h, c_b, c_c, r, k, v, dec_f, dec_b, iclr_f, iclr_b, g_dn,
         q_na, k_na, v_na, u_sg, v_sg) = jnp.split(proj, split_points, axis=-1)
        y_conv = c_b * dwconv3(c_c * c_h, conv_a_w[l])
        y_rwkv = rwkv7_mix(r, k, v, dec_f, dec_b, iclr_f, iclr_b, g_dn,
                           rwkv_w0[l], rwkv_w_up[l], rwkv_a0[l], rwkv_a_up[l], rwkv_g_up[l],
                           rwkv_k_k[l], rwkv_k_a[l], rwkv_r_k[l], rwkv_lnx_w[l], rwkv_lnx_b[l])
        y_na = neighborhood_attention(q_na, k_na, v_na, na_rpb[l])
        y_sgu = spatial_gating(u_sg, v_sg, sgu_norm[l], sgu_w[l], sgu_b[l])
        groups = jnp.stack([y_conv, y_rwkv, y_na, y_sgu], axis=2)
        merged = rms_norm(groups, 1.0).reshape(B, T, D_MIX) * merge_gain[l]
        x = x + rms_norm(merged @ w_out[l], norm_mix_post[l])
        h = rms_norm(x, norm_ffn_pre[l])
        gate, lin = jnp.split(h @ ffn_w_up[l], 2, axis=-1)
        hid = jax.nn.gelu(dwconv3(gate, ffn_conv[l])) * lin
        x = x + rms_norm(hid @ ffn_w_down[l], norm_ffn_post[l])
    return x
```

```python
import functools
import math

import jax
import jax.numpy as jnp
from jax import lax
from jax.experimental import pallas as pl
from jax.experimental.pallas import tpu as pltpu

F32 = jnp.float32
BF16 = jnp.bfloat16

D_MODEL = 1024
GRID_W = 64
HEADS = 4
HEAD_DIM = 64
GROUP_W = HEADS * HEAD_DIM
DECAY_RANK = 16
ICLR_RANK = 16
GATE_RANK = 32
LORA_W = 128
DECAY_SCALE = math.exp(-0.5)
GN_EPS = 64e-5
NA_KH = 8
NA_KW = 16
SGU_CHUNK = 128
D_FF = 2816
NORM_EPS = 1e-6
NEG_BIG = -1e30

WKV_CHUNK = 64
WKV_BLOCK = 128
ROW_TILE = 512
NA_ROWS = 8
VMEM_LIMIT = 56 * 1024 * 1024


def _dot(a, b):
    return jnp.dot(a.astype(BF16), b.astype(BF16), preferred_element_type=F32)


def _dot_nt(a, b):
    return lax.dot_general(a.astype(BF16), b.astype(BF16), (((1,), (1,)), ((), ())),
                           preferred_element_type=F32)


def _dot_tn(a, b):
    return lax.dot_general(a.astype(BF16), b.astype(BF16), (((0,), (0,)), ((), ())),
                           preferred_element_type=F32)


def _split(x):
    hi = x.astype(BF16)
    lo = (x - hi.astype(F32)).astype(BF16)
    return hi, lo


def _dot_x3(a, b):
    ah, al = _split(a)
    bh, bl = _split(b)
    d = lambda p, q: jnp.dot(p, q, preferred_element_type=F32)
    return d(ah, bh) + (d(ah, bl) + d(al, bh))


def _dot_exact_lhs(a_bf16, b):
    b1 = b.astype(BF16)
    r1 = b - b1.astype(F32)
    b2 = r1.astype(BF16)
    b3 = (r1 - b2.astype(F32)).astype(BF16)
    d = lambda q: jnp.dot(a_bf16, q, preferred_element_type=F32)
    return d(b1) + (d(b2) + d(b3))


def _dot_exact_rhs(a, b_bf16):
    a1 = a.astype(BF16)
    r1 = a - a1.astype(F32)
    a2 = r1.astype(BF16)
    a3 = (r1 - a2.astype(F32)).astype(BF16)
    d = lambda p: jnp.dot(p, b_bf16, preferred_element_type=F32)
    return d(a1) + (d(a2) + d(a3))


def _sigmoid(x):
    return 1.0 / (1.0 + jnp.exp(-x))


def _rms(x, eps=NORM_EPS):
    return x * lax.rsqrt(jnp.mean(x * x, axis=-1, keepdims=True) + eps)


def _full(shape):
    n = len(shape)
    return pl.BlockSpec(shape, lambda *_: (0,) * n)


IN_GROUPS = (3 * GROUP_W, 3 * GROUP_W, 3 * GROUP_W, 2 * GROUP_W, LORA_W)
IN_COLS = sum(IN_GROUPS)


def _inproj_body(x_ref, g_ref, w_ref, conv_ref, rkv_ref, na_ref, sgu_ref, lora_ref):
    h = _rms(x_ref[...]) * g_ref[...]
    p = jnp.dot(h.astype(BF16), w_ref[...], preferred_element_type=F32)
    o = 0
    for ref, wdt in zip((conv_ref, rkv_ref, na_ref, sgu_ref, lora_ref), IN_GROUPS):
        ref[...] = p[:, o:o + wdt].astype(ref.dtype)
        o += wdt


def _inproj(x2, gain, w):
    n = x2.shape[0]
    tm = ROW_TILE
    row = lambda wdt: pl.BlockSpec((tm, wdt), lambda i: (i, 0))
    return pl.pallas_call(
        _inproj_body,
        grid=(n // tm,),
        in_specs=[row(D_MODEL), _full((1, D_MODEL)), _full((D_MODEL, IN_COLS))],
        out_specs=[row(w_) for w_ in IN_GROUPS],
        out_shape=[jax.ShapeDtypeStruct((n, IN_GROUPS[0]), BF16),
                   jax.ShapeDtypeStruct((n, IN_GROUPS[1]), BF16),
                   jax.ShapeDtypeStruct((n, IN_GROUPS[2]), BF16),
                   jax.ShapeDtypeStruct((n, IN_GROUPS[3]), BF16),
                   jax.ShapeDtypeStruct((n, IN_GROUPS[4]), F32)],
        compiler_params=pltpu.CompilerParams(dimension_semantics=("parallel",),
                                             vmem_limit_bytes=VMEM_LIMIT),
        name="inproj",
    )(x2, gain, w)


def _wkv_chunk(rkv, lora, d, tri, seg, w0, wup, a0, aup, kk_gain, ka_gain, h_ref):
    c = WKV_CHUNK
    r = rkv[:, 0:GROUP_W].astype(F32)
    k = rkv[:, GROUP_W:2 * GROUP_W].astype(F32)
    v = rkv[:, 2 * GROUP_W:3 * GROUP_W].astype(F32)

    kk = k * kk_gain
    kk = kk * lax.rsqrt(_dot_exact_rhs(kk * kk, seg) + 1e-12)
    lw = -DECAY_SCALE * _sigmoid(w0 + _dot_x3(jnp.tanh(lora), wup))
    a_lr = _sigmoid(a0 + _dot_x3(lora, aup))
    k_eff = k * (1.0 + (a_lr - 1.0) * ka_gain)
    a_vec = -kk
    b_vec = kk * a_lr

    cum = _dot_exact_lhs(tri, lw)
    tot = cum[c - 1:c, :] if d == 0 else cum[0:1, :]
    g_inv = jnp.exp(-cum)
    g_hat = jnp.exp(tot - cum)
    a_t = a_vec * jnp.exp(cum - lw)
    r_t = r * jnp.exp(cum)
    b_t = b_vec * g_inv
    k_t = k_eff * g_inv
    b_h = b_vec * g_hat
    k_h = k_eff * g_hat
    g_end = jnp.exp(tot)

    row = lax.broadcasted_iota(jnp.int32, (c, c), 0)
    col = lax.broadcasted_iota(jnp.int32, (c, c), 1)
    strict = (col < row) if d == 0 else (col > row)
    incl = (col <= row) if d == 0 else (col >= row)
    diag = col == row
    zero = jnp.zeros((c, c), F32)

    ys = []
    for h in range(HEADS):
        sl = slice(h * HEAD_DIM, (h + 1) * HEAD_DIM)
        a_th, r_th, v_h = a_t[:, sl], r_t[:, sl], v[:, sl]
        m1 = _dot_nt(jnp.concatenate([a_th, r_th], axis=0),
                     jnp.concatenate([b_t[:, sl], k_t[:, sl]], axis=0))
        a_ab = jnp.where(strict, m1[:c, :c], 0.0)
        a_ak = jnp.where(strict, m1[:c, c:], 0.0)
        a_rb = jnp.where(incl, m1[c:, :c], 0.0)
        a_rk = jnp.where(incl, m1[c:, c:], 0.0)

        x = jnp.concatenate([a_th, _dot(a_ak, v_h)], axis=1)
        s = a_ab
        for j in range(6):
            if j < 5:
                sx = _dot_x3(s, jnp.concatenate([s, x], axis=1))
                s, x = sx[:, :c], x + sx[:, c:]
            else:
                x = x + _dot_x3(s, x)

        z = jnp.concatenate([x, jnp.concatenate([zero, v_h], axis=1)], axis=0)
        ry = _dot(jnp.concatenate([a_rb, a_rk], axis=1), z)
        pq = _dot_tn(jnp.concatenate([b_h[:, sl], k_h[:, sl]], axis=0), z)
        rhat = ry[:, :c] + r_th
        p = pq[:, :c] + jnp.where(diag, g_end[:, sl], 0.0)

        hs = h_ref[d, h]
        out = _dot_x3(jnp.concatenate([rhat, p], axis=0), hs)
        ys.append(out[:c] + ry[:, c:])
        h_ref[d, h] = out[c:] + pq[:, c:]
    return jnp.concatenate(ys, axis=1)


def _wkv_body(rkv_f, lora_f, rkv_b, lora_b, tri_ref, seg_ref, w0_ref, wup_ref, a0_ref, aup_ref,
              kk_ref, ka_ref, yf_ref, yb_ref, h_ref):
    @pl.when(pl.program_id(1) == 0)
    def _():
        h_ref[...] = jnp.zeros_like(h_ref)

    seg = seg_ref[...]
    nch = WKV_BLOCK // WKV_CHUNK
    for d, (rkv_ref, lora_ref, y_ref) in enumerate(((rkv_f, lora_f, yf_ref), (rkv_b, lora_b, yb_ref))):
        order = range(nch) if d == 0 else range(nch - 1, -1, -1)
        for ci in order:
            rows = slice(ci * WKV_CHUNK, (ci + 1) * WKV_CHUNK)
            y_ref[rows, :] = _wkv_chunk(rkv_ref[rows, :], lora_ref[rows, :], d, tri_ref[d], seg,
                                        w0_ref[d], wup_ref[d], a0_ref[d], aup_ref[d],
                                        kk_ref[...], ka_ref[...], h_ref)


def _wkv(rkv, lora, batch, tri, seg, w0, wup, a0, aup, k_k, k_a):
    n = rkv.shape[0]
    nb = n // batch // WKV_BLOCK
    fwd = lambda wdt: pl.BlockSpec((WKV_BLOCK, wdt), lambda b, j: (b * nb + j, 0))
    bwd = lambda wdt: pl.BlockSpec((WKV_BLOCK, wdt), lambda b, j: (b * nb + nb - 1 - j, 0))
    return pl.pallas_call(
        _wkv_body,
        grid=(batch, nb),
        in_specs=[fwd(3 * GROUP_W), fwd(LORA_W), bwd(3 * GROUP_W), bwd(LORA_W),
                  _full(tri.shape), _full(seg.shape), _full(w0.shape), _full(wup.shape),
                  _full(a0.shape), _full(aup.shape), _full(k_k.shape), _full(k_a.shape)],
        out_specs=[fwd(GROUP_W), bwd(GROUP_W)],
        out_shape=[jax.ShapeDtypeStruct((n, GROUP_W), F32)] * 2,
        scratch_shapes=[pltpu.VMEM((2, HEADS, HEAD_DIM, HEAD_DIM), F32)],
        compiler_params=pltpu.CompilerParams(dimension_semantics=("parallel", "arbitrary"),
                                             vmem_limit_bytes=VMEM_LIMIT),
        name="wkv7",
    )(rkv, lora, rkv, lora, tri, seg, w0, wup, a0, aup, k_k, k_a)


def _na_body(q_ref, k_ref, v_ref, bias_ref, o_ref, *, rows):
    j = pl.program_id(1)
    band = NA_KH * GRID_W
    lane = lax.broadcasted_iota(jnp.int32, (1, GROUP_W), 1) // HEAD_DIM
    for rr in range(NA_ROWS):
        r = j * NA_ROWS + rr
        rs = jnp.clip(r - NA_KH // 2, 0, rows - NA_KH)
        start = pl.multiple_of(rs * GRID_W, GRID_W)
        kb = k_ref[pl.ds(start, band), :]
        vb = v_ref[pl.ds(start, band), :]
        q = q_ref[rr * GRID_W:(rr + 1) * GRID_W, :].astype(F32) * (HEAD_DIM ** -0.5)
        didx = rs - r + (NA_KH - 1)
        acc = jnp.zeros((GRID_W, GROUP_W), F32)
        for h in range(HEADS):
            head = lane == h
            s = _dot_nt(jnp.where(head, q, 0.0), kb) + bias_ref[h, didx]
            p = jnp.exp(s - jnp.max(s, axis=-1, keepdims=True))
            o = _dot(p, vb) / jnp.sum(p, axis=-1, keepdims=True)
            acc = jnp.where(head, o, acc)
        o_ref[rr * GRID_W:(rr + 1) * GRID_W, :] = acc.astype(o_ref.dtype)


def _na(na, bias, batch):
    n = na.shape[0]
    t = n // batch
    rows = t // GRID_W
    nj = rows // NA_ROWS
    blk = NA_ROWS * GRID_W
    return pl.pallas_call(
        functools.partial(_na_body, rows=rows),
        grid=(batch, nj),
        in_specs=[pl.BlockSpec((blk, GROUP_W), lambda b, j: (b * nj + j, 0)),
                  pl.BlockSpec((t, GROUP_W), lambda b, j: (b, 1)),
                  pl.BlockSpec((t, GROUP_W), lambda b, j: (b, 2)),
                  _full(bias.shape)],
        out_specs=pl.BlockSpec((blk, GROUP_W), lambda b, j: (b * nj + j, 0)),
        out_shape=jax.ShapeDtypeStruct((n, GROUP_W), BF16),
        compiler_params=pltpu.CompilerParams(dimension_semantics=("parallel", "arbitrary"),
                                             vmem_limit_bytes=VMEM_LIMIT),
        name="nattn",
    )(na, na, na, bias)


def _na_bias_table(rpb, rows):
    kh = min(NA_KH, rows)
    c = jnp.arange(GRID_W)
    col_start = jnp.clip(c - NA_KW // 2, 0, GRID_W - NA_KW)
    col_mask = (c[None, :] >= col_start[:, None]) & (c[None, :] < col_start[:, None] + NA_KW)
    dx = jnp.clip(c[None, :] - c[:, None], -(NA_KW - 1), NA_KW - 1) + (NA_KW - 1)
    delta = jnp.arange(NA_KH)
    dy = delta[:, None] + jnp.arange(kh)[None, :]
    b = rpb[:, dy[:, None, :, None], dx[None, :, None, :]]
    b = jnp.where(col_mask[None, None, :, None, :], b, NEG_BIG)
    return b.reshape(HEADS, NA_KH, GRID_W, kh * GRID_W).astype(F32)


def _seg_mean(x, seg):
    return _dot_exact_rhs(x, seg) * (1.0 / HEAD_DIM)


def _merge_body(x_ref, conv_ref, cprev_ref, cnext_ref, rkv_ref, lora_ref, yf_ref, yb_ref, na_ref, sgu_ref,
                seg_ref, convw_ref, gup_ref, rk_ref, lnw_ref, lnb_ref, sgn_ref, sgw_ref, sgb_ref,
                mg_ref, wout_ref, gpost_ref, o_ref, *, tiles_per_seq):
    tm = ROW_TILE
    i = pl.program_id(0)
    seg = seg_ref[...]

    conv = conv_ref[...].astype(F32)
    z = conv[:, 2 * GROUP_W:] * conv[:, :GROUP_W]
    zp = cprev_ref[15:16, :].astype(F32)
    zp = zp[:, 2 * GROUP_W:] * zp[:, :GROUP_W]
    zn = cnext_ref[0:1, :].astype(F32)
    zn = zn[:, 2 * GROUP_W:] * zn[:, :GROUP_W]
    first = (i % tiles_per_seq) == 0
    last = (i % tiles_per_seq) == tiles_per_seq - 1
    zp = jnp.where(first, 0.0, zp)
    zn = jnp.where(last, 0.0, zn)
    ridx = lax.broadcasted_iota(jnp.int32, (tm, 1), 0)
    z_prev = jnp.where(ridx == 0, zp, pltpu.roll(z, 1, axis=0))
    z_next = jnp.where(ridx == tm - 1, zn, pltpu.roll(z, tm - 1, axis=0))
    cw = convw_ref[...]
    y_conv = conv[:, GROUP_W:2 * GROUP_W] * (z_prev * cw[0:1] + z * cw[1:2] + z_next * cw[2:3])

    rkv = rkv_ref[...].astype(F32)
    r, k, v = rkv[:, :GROUP_W], rkv[:, GROUP_W:2 * GROUP_W], rkv[:, 2 * GROUP_W:]
    y = yf_ref[...] + yb_ref[...]
    mu = _seg_mean(y, seg)
    yc = y - mu
    var = _seg_mean(yc * yc, seg)
    yn = yc * lax.rsqrt(var + GN_EPS) * lnw_ref[...] + lnb_ref[...]
    bonus = _dot_exact_rhs(r * k * rk_ref[...], seg) * v
    gate = _dot_x3(_sigmoid(lora_ref[...]), gup_ref[...])
    y_rwkv = (yn + bonus) * gate

    sg = sgu_ref[...].astype(F32)
    u = jax.nn.gelu(sg[:, :GROUP_W])
    gv = jax.nn.gelu(sg[:, GROUP_W:])
    gmu = jnp.mean(gv, axis=-1, keepdims=True)
    gc = gv - gmu
    gvn = gc * lax.rsqrt(jnp.mean(gc * gc, axis=-1, keepdims=True) + NORM_EPS) * sgn_ref[...]
    lane = lax.broadcasted_iota(jnp.int32, (1, GROUP_W), 1) // HEAD_DIM
    sgw = sgw_ref[...]
    mixed = []
    for ci in range(tm // SGU_CHUNK):
        res = _dot(sgw, gvn[ci * SGU_CHUNK:(ci + 1) * SGU_CHUNK])
        m = res[0:SGU_CHUNK]
        for h in range(1, HEADS):
            m = jnp.where(lane == h, res[h * SGU_CHUNK:(h + 1) * SGU_CHUNK], m)
        mixed.append(m + sgb_ref[...])
    y_sgu = u * jnp.concatenate(mixed, axis=0)

    y_na = na_ref[...].astype(F32)
    merged = jnp.concatenate([_rms(y_conv), _rms(y_rwkv), _rms(y_na), _rms(y_sgu)], axis=1) * mg_ref[...]
    out = jnp.dot(merged.astype(BF16), wout_ref[...], preferred_element_type=F32)
    o_ref[...] = x_ref[...] + _rms(out) * gpost_ref[...]


def _merge(x2, conv, rkv, lora, yf, yb, y_na, sgu, batch, seg, convw, gup, rk, lnw, lnb, sgn, sgw, sgb,
           mg, wout, gpost):
    n = x2.shape[0]
    tm = ROW_TILE
    tiles_per_seq = n // batch // tm
    nh = n // 16
    row = lambda wdt: pl.BlockSpec((tm, wdt), lambda i: (i, 0))
    prev = pl.BlockSpec((16, 3 * GROUP_W), lambda i: (jnp.maximum(i * (tm // 16) - 1, 0), 0))
    nxt = pl.BlockSpec((16, 3 * GROUP_W), lambda i: (jnp.minimum((i + 1) * (tm // 16), nh - 1), 0))
    params = (seg, convw, gup, rk, lnw, lnb, sgn, sgw, sgb, mg, wout, gpost)
    return pl.pallas_call(
        functools.partial(_merge_body, tiles_per_seq=tiles_per_seq),
        grid=(n // tm,),
        in_specs=[row(D_MODEL), row(3 * GROUP_W), prev, nxt, row(3 * GROUP_W), row(LORA_W),
                  row(GROUP_W), row(GROUP_W), row(GROUP_W), row(2 * GROUP_W)]
                 + [_full(p.shape) for p in params],
        out_specs=row(D_MODEL),
        out_shape=jax.ShapeDtypeStruct((n, D_MODEL), F32),
        compiler_params=pltpu.CompilerParams(dimension_semantics=("parallel",),
                                             vmem_limit_bytes=VMEM_LIMIT),
        name="merge",
    )(x2, conv, conv, conv, rkv, lora, yf, yb, y_na, sgu, *params)


FFN_SPLIT = 2
FFN_SLAB = D_FF // FFN_SPLIT
HALO = 8


def _ffn_body(x_ref, xprev_ref, xnext_ref, gpre_ref, wup_ref, cw_ref, wdn_ref, gpost_ref, o_ref, *,
              tiles_per_seq):
    tm = ROW_TILE
    i = pl.program_id(0)
    first = (i % tiles_per_seq) == 0
    last = (i % tiles_per_seq) == tiles_per_seq - 1
    x = x_ref[...]
    g = gpre_ref[...]
    hf = _rms(x) * g
    hp = jnp.where(first, 0.0, _rms(xprev_ref[...]) * g)
    hn = jnp.where(last, 0.0, _rms(xnext_ref[...]) * g)
    h = hf.astype(BF16)
    hext = jnp.concatenate([hp, hf, hn], axis=0).astype(BF16)
    ext = tm + 2 * HALO
    acc = jnp.zeros((tm, D_MODEL), F32)
    for s in range(FFN_SPLIT):
        c0 = s * FFN_SLAB
        gate = jnp.dot(hext, wup_ref[:, c0:c0 + FFN_SLAB], preferred_element_type=F32)
        lin = jnp.dot(h, wup_ref[:, D_FF + c0:D_FF + c0 + FFN_SLAB], preferred_element_type=F32)
        cw = cw_ref[:, c0:c0 + FFN_SLAB]
        g_prev = pltpu.roll(gate, 1, axis=0)[HALO:HALO + tm]
        g_next = pltpu.roll(gate, ext - 1, axis=0)[HALO:HALO + tm]
        cv = g_prev * cw[0:1] + gate[HALO:HALO + tm] * cw[1:2] + g_next * cw[2:3]
        hid = jax.nn.gelu(cv) * lin
        acc = acc + jnp.dot(hid.astype(BF16), wdn_ref[c0:c0 + FFN_SLAB, :], preferred_element_type=F32)
    o_ref[...] = x + _rms(acc) * gpost_ref[...]


def _ffn(x2, batch, gpre, wup, cw, wdn, gpost):
    n = x2.shape[0]
    tm = ROW_TILE
    tiles_per_seq = n // batch // tm
    nh = n // HALO
    row = pl.BlockSpec((tm, D_MODEL), lambda i: (i, 0))
    prev = pl.BlockSpec((HALO, D_MODEL), lambda i: (jnp.maximum(i * (tm // HALO) - 1, 0), 0))
    nxt = pl.BlockSpec((HALO, D_MODEL), lambda i: (jnp.minimum((i + 1) * (tm // HALO), nh - 1), 0))
    params = (gpre, wup, cw, wdn, gpost)
    return pl.pallas_call(
        functools.partial(_ffn_body, tiles_per_seq=tiles_per_seq),
        grid=(n // tm,),
        in_specs=[row, prev, nxt] + [_full(p.shape) for p in params],
        out_specs=row,
        out_shape=jax.ShapeDtypeStruct((n, D_MODEL), F32),
        compiler_params=pltpu.CompilerParams(dimension_semantics=("parallel",),
                                             vmem_limit_bytes=VMEM_LIMIT),
        name="convffn",
    )(x2, x2, x2, *params)


def _regroup_w_in(w):
    g = GROUP_W
    lora0 = 6 * g
    lora1 = lora0 + 2 * DECAY_RANK + 2 * ICLR_RANK + GATE_RANK
    pad = jnp.zeros((w.shape[0], LORA_W - (lora1 - lora0)), w.dtype)
    return jnp.concatenate([w[:, :lora0], w[:, lora1:lora1 + 5 * g], w[:, lora0:lora1], pad], axis=1).astype(BF16)


def _pad_rows(w, start):
    return jnp.zeros((LORA_W, w.shape[-1]), w.dtype).at[start:start + w.shape[0]].set(w)


def _layer_params(l, p, rows):
    wup = jnp.stack([_pad_rows(p["rwkv_w_up"][l, d], d * DECAY_RANK) for d in range(2)])
    aup = jnp.stack([_pad_rows(p["rwkv_a_up"][l, d], 2 * DECAY_RANK + d * ICLR_RANK) for d in range(2)])
    gup = _pad_rows(p["rwkv_g_up"][l], 2 * DECAY_RANK + 2 * ICLR_RANK)
    row2 = lambda a: a.reshape(1, -1)
    sgb = jnp.repeat(p["sgu_b"][l].T, HEAD_DIM, axis=1)
    return dict(
        w_in=_regroup_w_in(p["w_in"][l]), g_mix_pre=row2(p["norm_mix_pre"][l]),
        w0=p["rwkv_w0"][l].reshape(2, 1, GROUP_W), wup=wup, a0=p["rwkv_a0"][l].reshape(2, 1, GROUP_W), aup=aup,
        k_k=row2(p["rwkv_k_k"][l]), k_a=row2(p["rwkv_k_a"][l]),
        bias=_na_bias_table(p["na_rpb"][l], rows),
        convw=p["conv_a_w"][l], gup=gup, rk=row2(p["rwkv_r_k"][l]), lnw=row2(p["rwkv_lnx_w"][l]),
        lnb=row2(p["rwkv_lnx_b"][l]), sgn=row2(p["sgu_norm"][l]),
        sgw=p["sgu_w"][l].reshape(HEADS * SGU_CHUNK, SGU_CHUNK).astype(BF16), sgb=sgb,
        mg=row2(p["merge_gain"][l]), wout=p["w_out"][l].astype(BF16), g_mix_post=row2(p["norm_mix_post"][l]),
        g_ffn_pre=row2(p["norm_ffn_pre"][l]), ffn_up=p["ffn_w_up"][l].astype(BF16), ffn_conv=p["ffn_conv"][l],
        ffn_down=p["ffn_w_down"][l].astype(BF16), g_ffn_post=row2(p["norm_ffn_post"][l]),
    )


def _constants():
    c = WKV_CHUNK
    i = jnp.arange(c)
    tri = jnp.stack([i[None, :] <= i[:, None], i[None, :] >= i[:, None]]).astype(BF16)
    g = jnp.arange(GROUP_W) // HEAD_DIM
    seg = (g[:, None] == g[None, :]).astype(BF16)
    return tri, seg


def _forward(x, p):
    batch, t, _ = x.shape
    assert t % ROW_TILE == 0 and t % (NA_ROWS * GRID_W) == 0 and t // GRID_W >= NA_KH
    x2 = x.reshape(batch * t, D_MODEL)
    tri, seg = _constants()
    for l in range(p["w_in"].shape[0]):
        q = _layer_params(l, p, t // GRID_W)
        conv, rkv, na, sgu, lora = _inproj(x2, q["g_mix_pre"], q["w_in"])
        yf, yb = _wkv(rkv, lora, batch, tri, seg, q["w0"], q["wup"], q["a0"], q["aup"], q["k_k"], q["k_a"])
        y_na = _na(na, q["bias"], batch)
        x2 = _merge(x2, conv, rkv, lora, yf, yb, y_na, sgu, batch, seg, q["convw"], q["gup"], q["rk"],
                    q["lnw"], q["lnb"], q["sgn"], q["sgw"], q["sgb"], q["mg"], q["wout"], q["g_mix_post"])
        x2 = _ffn(x2, batch, q["g_ffn_pre"], q["ffn_up"], q["ffn_conv"], q["ffn_down"], q["g_ffn_post"])
    return x2.reshape(batch, t, D_MODEL)


def kernel(x, norm_mix_pre, norm_mix_post, norm_ffn_pre, norm_ffn_post, w_in, conv_a_w, rwkv_w0, rwkv_w_up,
           rwkv_a0, rwkv_a_up, rwkv_g_up, rwkv_k_k, rwkv_k_a, rwkv_r_k, rwkv_lnx_w, rwkv_lnx_b, na_rpb,
           sgu_norm, sgu_w, sgu_b, merge_gain, w_out, ffn_w_up, ffn_conv, ffn_w_down):
    p = dict(norm_mix_pre=norm_mix_pre, norm_mix_post=norm_mix_post, norm_ffn_pre=norm_ffn_pre,
             norm_ffn_post=norm_ffn_post, w_in=w_in, conv_a_w=conv_a_w, rwkv_w0=rwkv_w0, rwkv_w_up=rwkv_w_up,
             rwkv_a0=rwkv_a0, rwkv_a_up=rwkv_a_up, rwkv_g_up=rwkv_g_up, rwkv_k_k=rwkv_k_k, rwkv_k_a=rwkv_k_a,
             rwkv_r_k=rwkv_r_k, rwkv_lnx_w=rwkv_lnx_w, rwkv_lnx_b=rwkv_lnx_b, na_rpb=na_rpb, sgu_norm=sgu_norm,
             sgu_w=sgu_w, sgu_b=sgu_b, merge_gain=merge_gain, w_out=w_out, ffn_w_up=ffn_w_up, ffn_conv=ffn_conv,
             ffn_w_down=ffn_w_down)
    return _forward(x, p)
```

```python
import functools
import math

import jax
import jax.numpy as jnp
from jax import lax
from jax.experimental import pallas as pl
from jax.experimental.pallas import tpu as pltpu

F32 = jnp.float32
BF16 = jnp.bfloat16

D_MODEL = 1024
GRID_W = 64
HEADS = 4
HEAD_DIM = 64
GROUP_W = HEADS * HEAD_DIM
DECAY_RANK = 16
ICLR_RANK = 16
GATE_RANK = 32
LORA_W = 128
DECAY_SCALE = math.exp(-0.5)
GN_EPS = 64e-5
NA_KH = 8
NA_KW = 16
SGU_CHUNK = 128
D_FF = 2816
NORM_EPS = 1e-6
NEG_BIG = -1e30

WKV_CHUNK = 64
WKV_BLOCK = 128
ROW_TILE = 512
NA_ROWS = 8
VMEM_LIMIT = 56 * 1024 * 1024


def _dot(a, b):
    return jnp.dot(a.astype(BF16), b.astype(BF16), preferred_element_type=F32)


def _dot_nt(a, b):
    return lax.dot_general(a.astype(BF16), b.astype(BF16), (((1,), (1,)), ((), ())),
                           preferred_element_type=F32)


def _dot_tn(a, b):
    return lax.dot_general(a.astype(BF16), b.astype(BF16), (((0,), (0,)), ((), ())),
                           preferred_element_type=F32)


def _split(x):
    hi = x.astype(BF16)
    lo = (x - hi.astype(F32)).astype(BF16)
    return hi, lo


def _dot_x3(a, b):
    ah, al = _split(a)
    bh, bl = _split(b)
    d = lambda p, q: jnp.dot(p, q, preferred_element_type=F32)
    return d(ah, bh) + (d(ah, bl) + d(al, bh))


def _dot_exact_lhs(a_bf16, b):
    b1 = b.astype(BF16)
    r1 = b - b1.astype(F32)
    b2 = r1.astype(BF16)
    b3 = (r1 - b2.astype(F32)).astype(BF16)
    d = lambda q: jnp.dot(a_bf16, q, preferred_element_type=F32)
    return d(b1) + (d(b2) + d(b3))


def _dot_exact_rhs(a, b_bf16):
    a1 = a.astype(BF16)
    r1 = a - a1.astype(F32)
    a2 = r1.astype(BF16)
    a3 = (r1 - a2.astype(F32)).astype(BF16)
    d = lambda p: jnp.dot(p, b_bf16, preferred_element_type=F32)
    return d(a1) + (d(a2) + d(a3))


def _sigmoid(x):
    return 1.0 / (1.0 + jnp.exp(-x))


def _rms(x, eps=NORM_EPS):
    return x * lax.rsqrt(jnp.mean(x * x, axis=-1, keepdims=True) + eps)


def _full(shape):
    n = len(shape)
    return pl.BlockSpec(shape, lambda *_: (0,) * n)


IN_GROUPS = (3 * GROUP_W, 3 * GROUP_W, 3 * GROUP_W, 2 * GROUP_W, LORA_W)
IN_COLS = sum(IN_GROUPS)


def _inproj_body(x_ref, g_ref, w_ref, conv_ref, rkv_ref, na_ref, sgu_ref, lora_ref):
    h = _rms(x_ref[...]) * g_ref[...]
    p = jnp.dot(h.astype(BF16), w_ref[...], preferred_element_type=F32)
    o = 0
    for ref, wdt in zip((conv_ref, rkv_ref, na_ref, sgu_ref, lora_ref), IN_GROUPS):
        ref[...] = p[:, o:o + wdt].astype(ref.dtype)
        o += wdt


def _inproj(x2, gain, w):
    n = x2.shape[0]
    tm = ROW_TILE
    row = lambda wdt: pl.BlockSpec((tm, wdt), lambda i: (i, 0))
    return pl.pallas_call(
        _inproj_body,
        grid=(n // tm,),
        in_specs=[row(D_MODEL), _full((1, D_MODEL)), _full((D_MODEL, IN_COLS))],
        out_specs=[row(w_) for w_ in IN_GROUPS],
        out_shape=[jax.ShapeDtypeStruct((n, IN_GROUPS[0]), BF16),
                   jax.ShapeDtypeStruct((n, IN_GROUPS[1]), BF16),
                   jax.ShapeDtypeStruct((n, IN_GROUPS[2]), BF16),
                   jax.ShapeDtypeStruct((n, IN_GROUPS[3]), BF16),
                   jax.ShapeDtypeStruct((n, IN_GROUPS[4]), F32)],
        compiler_params=pltpu.CompilerParams(dimension_semantics=("parallel",),
                                             vmem_limit_bytes=VMEM_LIMIT),
        name="inproj",
    )(x2, gain, w)


PAIR_W = 2 * HEAD_DIM
N_PAIRS = HEADS // 2
SOLVE_LEVELS = 6


def _mm(a, b):
    return jnp.dot(a, b, preferred_element_type=F32)


def _mm_nt(a, b):
    return lax.dot_general(a, b, (((1,), (1,)), ((), ())), preferred_element_type=F32)


def _mm_tn(a, b):
    return lax.dot_general(a, b, (((0,), (0,)), ((), ())), preferred_element_type=F32)


def _mm_x3(a, b):
    return _mm(a[0], b[0]) + (_mm(a[0], b[1]) + _mm(a[1], b[0]))


def _wkv_prep(rkv, lora, tri, seg, w0, wup, a0, aup, kk_gain, ka_gain, d):
    c = WKV_CHUNK
    nch = rkv.shape[0] // c
    r = rkv[:, 0:GROUP_W].astype(F32)
    k = rkv[:, GROUP_W:2 * GROUP_W].astype(F32)
    v = rkv[:, 2 * GROUP_W:3 * GROUP_W]

    kk = k * kk_gain
    kk = kk * lax.rsqrt(_dot_exact_rhs(kk * kk, seg) + 1e-12)
    lw = -DECAY_SCALE * _sigmoid(w0 + _dot_x3(jnp.tanh(lora), wup))
    a_lr = _sigmoid(a0 + _dot_x3(lora, aup))
    k_eff = k * (1.0 + (a_lr - 1.0) * ka_gain)
    b_vec = kk * a_lr

    cum = _dot_exact_lhs(tri, lw)
    ends = [cum[ci * c + c - 1:ci * c + c] if d == 0 else cum[ci * c:ci * c + 1] for ci in range(nch)]
    tot = jnp.concatenate([jnp.broadcast_to(e, (c, GROUP_W)) for e in ends], axis=0)
    g_inv = jnp.exp(-cum)
    g_hat = jnp.exp(tot - cum)
    return dict(
        a_t=-kk * jnp.exp(cum - lw), r_t=r * jnp.exp(cum),
        b_t=(b_vec * g_inv).astype(BF16), k_t=(k_eff * g_inv).astype(BF16),
        b_h=(b_vec * g_hat).astype(BF16), k_h=(k_eff * g_hat).astype(BF16),
        v=v, g_end=[jnp.exp(e) for e in ends])


def _wkv_body(rkv_f, lora_f, rkv_b, lora_b, tri_ref, seg_ref, w0_ref, wup_ref, a0_ref, aup_ref,
              kk_ref, ka_ref, yf_ref, yb_ref, h_ref):
    @pl.when(pl.program_id(1) == 0)
    def _():
        h_ref[...] = jnp.zeros_like(h_ref)

    c = WKV_CHUNK
    nch = WKV_BLOCK // c
    seg = seg_ref[...]
    prep = [_wkv_prep(rkv_ref[...], lora_ref[...], tri_ref[d], seg, w0_ref[d], wup_ref[d], a0_ref[d],
                      aup_ref[d], kk_ref[...], ka_ref[...], d)
            for d, (rkv_ref, lora_ref) in enumerate(((rkv_f, lora_f), (rkv_b, lora_b)))]

    lane1 = lax.broadcasted_iota(jnp.int32, (1, PAIR_W), 1)
    lane2 = lax.broadcasted_iota(jnp.int32, (1, 2 * PAIR_W), 1)
    head1 = [lane1 // HEAD_DIM == h for h in range(2)]
    head2 = [(lane2 % PAIR_W) // HEAD_DIM == h for h in range(2)]
    row = lax.broadcasted_iota(jnp.int32, (PAIR_W, PAIR_W), 0)
    col = lax.broadcasted_iota(jnp.int32, (PAIR_W, PAIR_W), 1)
    t_idx, s_idx = row % c, col % c
    tri_mask = [(s_idx < t_idx) | ((row >= c) & (s_idx == t_idx)),
                (s_idx > t_idx) | ((row >= c) & (s_idx == t_idx))]
    diag = row == col
    top_rows = lax.broadcasted_iota(jnp.int32, (PAIR_W, 1), 0) < c
    zeros_pair = jnp.zeros((c, PAIR_W), BF16)

    probs = [(d, ci, p) for d in range(2) for ci in range(nch) for p in range(N_PAIRS)]
    sl = lambda ci, p: (slice(ci * c, (ci + 1) * c), slice(p * PAIR_W, (p + 1) * PAIR_W))

    a4 = {}
    for d, ci, p in probs:
        q = prep[d]
        rows, lanes = sl(ci, p)
        lhs = jnp.concatenate([q["a_t"][rows, lanes], q["r_t"][rows, lanes]], axis=0)
        rhs = jnp.concatenate([q["b_t"][rows, lanes], q["k_t"][rows, lanes]], axis=0)
        for h in range(2):
            m1 = _mm_nt(jnp.where(head1[h], lhs, 0.0).astype(BF16), rhs)
            a4[d, ci, p, h] = jnp.where(tri_mask[d], m1, 0.0)

    xs, s128 = {}, {}
    for d, ci, p in probs:
        q = prep[d]
        rows, lanes = sl(ci, p)
        v0 = jnp.concatenate([zeros_pair, q["v"][rows, lanes]], axis=0)
        for h in range(2):
            top = a4[d, ci, p, h][:c]
            akv = _mm(top.astype(BF16), v0)
            xs[d, ci, p, h] = jnp.concatenate([jnp.where(head1[h], q["a_t"][rows, lanes], 0.0),
                                               jnp.where(head1[h], akv, 0.0)], axis=1)
            s128[d, ci, p, h] = jnp.where(lane1 < c, top, 0.0)

    keys = [(d, ci, p, h) for d, ci, p in probs for h in range(2)]
    for j in range(SOLVE_LEVELS):
        for key in keys:
            s_op = s128[key][:, :c].astype(BF16)
            if j < SOLVE_LEVELS - 1:
                out = _mm(s_op, jnp.concatenate([s128[key], xs[key]], axis=1).astype(BF16))
                s128[key] = out[:, :PAIR_W]
                xs[key] = xs[key] + out[:, PAIR_W:]
            else:
                xs[key] = xs[key] + _mm(s_op, xs[key].astype(BF16))

    maps = {}
    for d, ci, p in probs:
        q = prep[d]
        rows, lanes = sl(ci, p)
        bk = jnp.concatenate([q["b_h"][rows, lanes], q["k_h"][rows, lanes]], axis=0)
        ry, pq = [], []
        for h in range(2):
            key = (d, ci, p, h)
            vz = jnp.concatenate([zeros_pair, jnp.where(head1[h], q["v"][rows, lanes], 0.0).astype(BF16)], axis=1)
            z = jnp.concatenate([xs[key].astype(BF16), vz], axis=0)
            ry.append(_mm(a4[key][c:].astype(BF16), z))
            pq.append(_mm_tn(bk, z))
        ry = ry[0] + ry[1]
        pq = jnp.where(top_rows, pq[0], pq[1])
        rhat = q["r_t"][rows, lanes] + ry[:, :PAIR_W]
        pmat = pq[:, :PAIR_W] + jnp.where(diag, q["g_end"][ci][:, lanes], 0.0)
        maps[d, ci, p] = (jnp.concatenate([rhat, pmat], axis=0), ry[:, PAIR_W:], pq[:, PAIR_W:])

    for step in range(nch):
        for d, y_ref in enumerate((yf_ref, yb_ref)):
            ci = step if d == 0 else nch - 1 - step
            for p in range(N_PAIRS):
                rows, lanes = sl(ci, p)
                lhs, yi, qmat = maps[d, ci, p]
                out = _mm_x3(_split(lhs), _split(h_ref[d, p]))
                y_ref[rows, lanes] = out[:c] + yi
                h_ref[d, p] = out[c:] + qmat


def _wkv(rkv, lora, batch, tri, seg, w0, wup, a0, aup, k_k, k_a):
    n = rkv.shape[0]
    nb = n // batch // WKV_BLOCK
    fwd = lambda wdt: pl.BlockSpec((WKV_BLOCK, wdt), lambda b, j: (b * nb + j, 0))
    bwd = lambda wdt: pl.BlockSpec((WKV_BLOCK, wdt), lambda b, j: (b * nb + nb - 1 - j, 0))
    return pl.pallas_call(
        _wkv_body,
        grid=(batch, nb),
        in_specs=[fwd(3 * GROUP_W), fwd(LORA_W), bwd(3 * GROUP_W), bwd(LORA_W),
                  _full(tri.shape), _full(seg.shape), _full(w0.shape), _full(wup.shape),
                  _full(a0.shape), _full(aup.shape), _full(k_k.shape), _full(k_a.shape)],
        out_specs=[fwd(GROUP_W), bwd(GROUP_W)],
        out_shape=[jax.ShapeDtypeStruct((n, GROUP_W), F32)] * 2,
        scratch_shapes=[pltpu.VMEM((2, N_PAIRS, PAIR_W, PAIR_W), F32)],
        compiler_params=pltpu.CompilerParams(dimension_semantics=("parallel", "arbitrary"),
                                             vmem_limit_bytes=VMEM_LIMIT),
        name="wkv7",
    )(rkv, lora, rkv, lora, tri, seg, w0, wup, a0, aup, k_k, k_a)


def _na_body(q_ref, k_ref, v_ref, bias_ref, o_ref, *, rows):
    j = pl.program_id(1)
    band = NA_KH * GRID_W
    lane = lax.broadcasted_iota(jnp.int32, (1, GROUP_W), 1) // HEAD_DIM
    starts, scores = [], []
    for rr in range(NA_ROWS):
        r = j * NA_ROWS + rr
        rs = jnp.clip(r - NA_KH // 2, 0, rows - NA_KH)
        start = pl.multiple_of(rs * GRID_W, GRID_W)
        q = q_ref[rr * GRID_W:(rr + 1) * GRID_W, :] * (HEAD_DIM ** -0.5)
        qs = jnp.concatenate([jnp.where(lane == h, q, 0.0) for h in range(HEADS)], axis=0).astype(BF16)
        s = _mm_nt(qs, k_ref[pl.ds(start, band), :])
        starts.append(start)
        scores.append(s + bias_ref[rs - r + (NA_KH - 1)])
    probs, norms = [], []
    for s in scores:
        p = jnp.exp(s - jnp.max(s, axis=-1, keepdims=True))
        norms.append(jnp.sum(p, axis=-1, keepdims=True))
        probs.append(p.astype(BF16))
    for rr in range(NA_ROWS):
        o = _mm(probs[rr], v_ref[pl.ds(starts[rr], band), :]) / norms[rr]
        acc = o[0:GRID_W]
        for h in range(1, HEADS):
            acc = jnp.where(lane == h, o[h * GRID_W:(h + 1) * GRID_W], acc)
        o_ref[rr * GRID_W:(rr + 1) * GRID_W, :] = acc.astype(o_ref.dtype)


def _na(na, bias, batch):
    n = na.shape[0]
    t = n // batch
    rows = t // GRID_W
    nj = rows // NA_ROWS
    blk = NA_ROWS * GRID_W
    return pl.pallas_call(
        functools.partial(_na_body, rows=rows),
        grid=(batch, nj),
        in_specs=[pl.BlockSpec((blk, GROUP_W), lambda b, j: (b * nj + j, 0)),
                  pl.BlockSpec((t, GROUP_W), lambda b, j: (b, 1)),
                  pl.BlockSpec((t, GROUP_W), lambda b, j: (b, 2)),
                  _full(bias.shape)],
        out_specs=pl.BlockSpec((blk, GROUP_W), lambda b, j: (b * nj + j, 0)),
        out_shape=jax.ShapeDtypeStruct((n, GROUP_W), BF16),
        compiler_params=pltpu.CompilerParams(dimension_semantics=("parallel", "arbitrary"),
                                             vmem_limit_bytes=VMEM_LIMIT),
        name="nattn",
    )(na, na, na, bias)


def _na_bias_table(rpb, rows):
    kh = min(NA_KH, rows)
    c = jnp.arange(GRID_W)
    col_start = jnp.clip(c - NA_KW // 2, 0, GRID_W - NA_KW)
    col_mask = (c[None, :] >= col_start[:, None]) & (c[None, :] < col_start[:, None] + NA_KW)
    dx = jnp.clip(c[None, :] - c[:, None], -(NA_KW - 1), NA_KW - 1) + (NA_KW - 1)
    onehot = (dx[None] == jnp.arange(2 * NA_KW - 1)[:, None, None]).astype(F32)
    by_dy = jnp.einsum("hyx,xqw->hyqw", rpb, onehot, precision=lax.Precision.HIGHEST)
    by_dy = jnp.where(col_mask[None, None], by_dy, NEG_BIG)
    b = jnp.stack([by_dy[:, delta:delta + kh] for delta in range(NA_KH)], axis=0)
    b = jnp.transpose(b, (0, 1, 3, 2, 4))
    return b.reshape(NA_KH, HEADS * GRID_W, kh * GRID_W).astype(F32)


def _seg_mean(x, seg):
    return _dot_exact_rhs(x, seg) * (1.0 / HEAD_DIM)


def _merge_body(x_ref, conv_ref, cprev_ref, cnext_ref, rkv_ref, lora_ref, yf_ref, yb_ref, na_ref, sgu_ref,
                seg_ref, convw_ref, gup_ref, rk_ref, lnw_ref, lnb_ref, sgn_ref, sgw_ref, sgb_ref,
                mg_ref, wout_ref, gpost_ref, o_ref, *, tiles_per_seq):
    tm = ROW_TILE
    i = pl.program_id(0)
    seg = seg_ref[...]

    conv = conv_ref[...].astype(F32)
    z = conv[:, 2 * GROUP_W:] * conv[:, :GROUP_W]
    zp = cprev_ref[15:16, :].astype(F32)
    zp = zp[:, 2 * GROUP_W:] * zp[:, :GROUP_W]
    zn = cnext_ref[0:1, :].astype(F32)
    zn = zn[:, 2 * GROUP_W:] * zn[:, :GROUP_W]
    first = (i % tiles_per_seq) == 0
    last = (i % tiles_per_seq) == tiles_per_seq - 1
    zp = jnp.where(first, 0.0, zp)
    zn = jnp.where(last, 0.0, zn)
    ridx = lax.broadcasted_iota(jnp.int32, (tm, 1), 0)
    z_prev = jnp.where(ridx == 0, zp, pltpu.roll(z, 1, axis=0))
    z_next = jnp.where(ridx == tm - 1, zn, pltpu.roll(z, tm - 1, axis=0))
    cw = convw_ref[...]
    y_conv = conv[:, GROUP_W:2 * GROUP_W] * (z_prev * cw[0:1] + z * cw[1:2] + z_next * cw[2:3])

    rkv = rkv_ref[...].astype(F32)
    r, k, v = rkv[:, :GROUP_W], rkv[:, GROUP_W:2 * GROUP_W], rkv[:, 2 * GROUP_W:]
    y = yf_ref[...] + yb_ref[...]
    mu = _seg_mean(y, seg)
    yc = y - mu
    var = _seg_mean(yc * yc, seg)
    yn = yc * lax.rsqrt(var + GN_EPS) * lnw_ref[...] + lnb_ref[...]
    bonus = _dot_exact_rhs(r * k * rk_ref[...], seg) * v
    gate = _dot_x3(_sigmoid(lora_ref[...]), gup_ref[...])
    y_rwkv = (yn + bonus) * gate

    sg = sgu_ref[...].astype(F32)
    u = jax.nn.gelu(sg[:, :GROUP_W])
    gv = jax.nn.gelu(sg[:, GROUP_W:])
    gmu = jnp.mean(gv, axis=-1, keepdims=True)
    gc = gv - gmu
    gvn = gc * lax.rsqrt(jnp.mean(gc * gc, axis=-1, keepdims=True) + NORM_EPS) * sgn_ref[...]
    lane = lax.broadcasted_iota(jnp.int32, (1, GROUP_W), 1) // HEAD_DIM
    sgw = sgw_ref[...]
    mixed = []
    for ci in range(tm // SGU_CHUNK):
        res = _dot(sgw, gvn[ci * SGU_CHUNK:(ci + 1) * SGU_CHUNK])
        m = res[0:SGU_CHUNK]
        for h in range(1, HEADS):
            m = jnp.where(lane == h, res[h * SGU_CHUNK:(h + 1) * SGU_CHUNK], m)
        mixed.append(m + sgb_ref[...])
    y_sgu = u * jnp.concatenate(mixed, axis=0)

    y_na = na_ref[...].astype(F32)
    merged = jnp.concatenate([_rms(y_conv), _rms(y_rwkv), _rms(y_na), _rms(y_sgu)], axis=1) * mg_ref[...]
    out = jnp.dot(merged.astype(BF16), wout_ref[...], preferred_element_type=F32)
    o_ref[...] = x_ref[...] + _rms(out) * gpost_ref[...]


def _merge(x2, conv, rkv, lora, yf, yb, y_na, sgu, batch, seg, convw, gup, rk, lnw, lnb, sgn, sgw, sgb,
           mg, wout, gpost):
    n = x2.shape[0]
    tm = ROW_TILE
    tiles_per_seq = n // batch // tm
    nh = n // 16
    row = lambda wdt: pl.BlockSpec((tm, wdt), lambda i: (i, 0))
    prev = pl.BlockSpec((16, 3 * GROUP_W), lambda i: (jnp.maximum(i * (tm // 16) - 1, 0), 0))
    nxt = pl.BlockSpec((16, 3 * GROUP_W), lambda i: (jnp.minimum((i + 1) * (tm // 16), nh - 1), 0))
    params = (seg, convw, gup, rk, lnw, lnb, sgn, sgw, sgb, mg, wout, gpost)
    return pl.pallas_call(
        functools.partial(_merge_body, tiles_per_seq=tiles_per_seq),
        grid=(n // tm,),
        in_specs=[row(D_MODEL), row(3 * GROUP_W), prev, nxt, row(3 * GROUP_W), row(LORA_W),
                  row(GROUP_W), row(GROUP_W), row(GROUP_W), row(2 * GROUP_W)]
                 + [_full(p.shape) for p in params],
        out_specs=row(D_MODEL),
        out_shape=jax.ShapeDtypeStruct((n, D_MODEL), F32),
        compiler_params=pltpu.CompilerParams(dimension_semantics=("parallel",),
                                             vmem_limit_bytes=VMEM_LIMIT),
        name="merge",
    )(x2, conv, conv, conv, rkv, lora, yf, yb, y_na, sgu, *params)


FFN_SPLIT = 2
FFN_SLAB = D_FF // FFN_SPLIT
HALO = 8


def _ffn_body(x_ref, xprev_ref, xnext_ref, gpre_ref, wup_ref, cw_ref, wdn_ref, gpost_ref, o_ref, *,
              tiles_per_seq):
    tm = ROW_TILE
    i = pl.program_id(0)
    first = (i % tiles_per_seq) == 0
    last = (i % tiles_per_seq) == tiles_per_seq - 1
    x = x_ref[...]
    g = gpre_ref[...]
    hf = _rms(x) * g
    hp = jnp.where(first, 0.0, _rms(xprev_ref[...]) * g)
    hn = jnp.where(last, 0.0, _rms(xnext_ref[...]) * g)
    h = hf.astype(BF16)
    hext = jnp.concatenate([hp, hf, hn], axis=0).astype(BF16)
    ext = tm + 2 * HALO
    acc = jnp.zeros((tm, D_MODEL), F32)
    for s in range(FFN_SPLIT):
        c0 = s * FFN_SLAB
        gate = jnp.dot(hext, wup_ref[:, c0:c0 + FFN_SLAB], preferred_element_type=F32)
        lin = jnp.dot(h, wup_ref[:, D_FF + c0:D_FF + c0 + FFN_SLAB], preferred_element_type=F32)
        cw = cw_ref[:, c0:c0 + FFN_SLAB]
        g_prev = pltpu.roll(gate, 1, axis=0)[HALO:HALO + tm]
        g_next = pltpu.roll(gate, ext - 1, axis=0)[HALO:HALO + tm]
        cv = g_prev * cw[0:1] + gate[HALO:HALO + tm] * cw[1:2] + g_next * cw[2:3]
        hid = jax.nn.gelu(cv) * lin
        acc = acc + jnp.dot(hid.astype(BF16), wdn_ref[c0:c0 + FFN_SLAB, :], preferred_element_type=F32)
    o_ref[...] = x + _rms(acc) * gpost_ref[...]


def _ffn(x2, batch, gpre, wup, cw, wdn, gpost):
    n = x2.shape[0]
    tm = ROW_TILE
    tiles_per_seq = n // batch // tm
    nh = n // HALO
    row = pl.BlockSpec((tm, D_MODEL), lambda i: (i, 0))
    prev = pl.BlockSpec((HALO, D_MODEL), lambda i: (jnp.maximum(i * (tm // HALO) - 1, 0), 0))
    nxt = pl.BlockSpec((HALO, D_MODEL), lambda i: (jnp.minimum((i + 1) * (tm // HALO), nh - 1), 0))
    params = (gpre, wup, cw, wdn, gpost)
    return pl.pallas_call(
        functools.partial(_ffn_body, tiles_per_seq=tiles_per_seq),
        grid=(n // tm,),
        in_specs=[row, prev, nxt] + [_full(p.shape) for p in params],
        out_specs=row,
        out_shape=jax.ShapeDtypeStruct((n, D_MODEL), F32),
        compiler_params=pltpu.CompilerParams(dimension_semantics=("parallel",),
                                             vmem_limit_bytes=VMEM_LIMIT),
        name="convffn",
    )(x2, x2, x2, *params)


def _regroup_w_in(w):
    g = GROUP_W
    lora0 = 6 * g
    lora1 = lora0 + 2 * DECAY_RANK + 2 * ICLR_RANK + GATE_RANK
    pad = jnp.zeros((w.shape[0], LORA_W - (lora1 - lora0)), w.dtype)
    return jnp.concatenate([w[:, :lora0], w[:, lora1:lora1 + 5 * g], w[:, lora0:lora1], pad], axis=1).astype(BF16)


def _pad_rows(w, start):
    return jnp.zeros((LORA_W, w.shape[-1]), w.dtype).at[start:start + w.shape[0]].set(w)


def _layer_params(l, p, rows):
    wup = jnp.stack([_pad_rows(p["rwkv_w_up"][l, d], d * DECAY_RANK) for d in range(2)])
    aup = jnp.stack([_pad_rows(p["rwkv_a_up"][l, d], 2 * DECAY_RANK + d * ICLR_RANK) for d in range(2)])
    gup = _pad_rows(p["rwkv_g_up"][l], 2 * DECAY_RANK + 2 * ICLR_RANK)
    row2 = lambda a: a.reshape(1, -1)
    sgb = jnp.repeat(p["sgu_b"][l].T, HEAD_DIM, axis=1)
    return dict(
        w_in=_regroup_w_in(p["w_in"][l]), g_mix_pre=row2(p["norm_mix_pre"][l]),
        w0=p["rwkv_w0"][l].reshape(2, 1, GROUP_W), wup=wup, a0=p["rwkv_a0"][l].reshape(2, 1, GROUP_W), aup=aup,
        k_k=row2(p["rwkv_k_k"][l]), k_a=row2(p["rwkv_k_a"][l]),
        bias=_na_bias_table(p["na_rpb"][l], rows),
        convw=p["conv_a_w"][l], gup=gup, rk=row2(p["rwkv_r_k"][l]), lnw=row2(p["rwkv_lnx_w"][l]),
        lnb=row2(p["rwkv_lnx_b"][l]), sgn=row2(p["sgu_norm"][l]),
        sgw=p["sgu_w"][l].reshape(HEADS * SGU_CHUNK, SGU_CHUNK).astype(BF16), sgb=sgb,
        mg=row2(p["merge_gain"][l]), wout=p["w_out"][l].astype(BF16), g_mix_post=row2(p["norm_mix_post"][l]),
        g_ffn_pre=row2(p["norm_ffn_pre"][l]), ffn_up=p["ffn_w_up"][l].astype(BF16), ffn_conv=p["ffn_conv"][l],
        ffn_down=p["ffn_w_down"][l].astype(BF16), g_ffn_post=row2(p["norm_ffn_post"][l]),
    )


def _constants():
    i = jnp.arange(WKV_BLOCK)
    same = (i[None, :] // WKV_CHUNK) == (i[:, None] // WKV_CHUNK)
    tri = jnp.stack([same & (i[None, :] <= i[:, None]),
                     same & (i[None, :] >= i[:, None])]).astype(BF16)
    g = jnp.arange(GROUP_W) // HEAD_DIM
    seg = (g[:, None] == g[None, :]).astype(BF16)
    return tri, seg


def _forward(x, p):
    batch, t, _ = x.shape
    assert t % ROW_TILE == 0 and t % (NA_ROWS * GRID_W) == 0 and t // GRID_W >= NA_KH
    x2 = x.reshape(batch * t, D_MODEL)
    tri, seg = _constants()
    for l in range(p["w_in"].shape[0]):
        q = _layer_params(l, p, t // GRID_W)
        conv, rkv, na, sgu, lora = _inproj(x2, q["g_mix_pre"], q["w_in"])
        yf, yb = _wkv(rkv, lora, batch, tri, seg, q["w0"], q["wup"], q["a0"], q["aup"], q["k_k"], q["k_a"])
        y_na = _na(na, q["bias"], batch)
        x2 = _merge(x2, conv, rkv, lora, yf, yb, y_na, sgu, batch, seg, q["convw"], q["gup"], q["rk"],
                    q["lnw"], q["lnb"], q["sgn"], q["sgw"], q["sgb"], q["mg"], q["wout"], q["g_mix_post"])
        x2 = _ffn(x2, batch, q["g_ffn_pre"], q["ffn_up"], q["ffn_conv"], q["ffn_down"], q["g_ffn_post"])
    return x2.reshape(batch, t, D_MODEL)


def kernel(x, norm_mix_pre, norm_mix_post, norm_ffn_pre, norm_ffn_post, w_in, conv_a_w, rwkv_w0, rwkv_w_up,
           rwkv_a0, rwkv_a_up, rwkv_g_up, rwkv_k_k, rwkv_k_a, rwkv_r_k, rwkv_lnx_w, rwkv_lnx_b, na_rpb,
           sgu_norm, sgu_w, sgu_b, merge_gain, w_out, ffn_w_up, ffn_conv, ffn_w_down):
    p = dict(norm_mix_pre=norm_mix_pre, norm_mix_post=norm_mix_post, norm_ffn_pre=norm_ffn_pre,
             norm_ffn_post=norm_ffn_post, w_in=w_in, conv_a_w=conv_a_w, rwkv_w0=rwkv_w0, rwkv_w_up=rwkv_w_up,
             rwkv_a0=rwkv_a0, rwkv_a_up=rwkv_a_up, rwkv_g_up=rwkv_g_up, rwkv_k_k=rwkv_k_k, rwkv_k_a=rwkv_k_a,
             rwkv_r_k=rwkv_r_k, rwkv_lnx_w=rwkv_lnx_w, rwkv_lnx_b=rwkv_lnx_b, na_rpb=na_rpb, sgu_norm=sgu_norm,
             sgu_w=sgu_w, sgu_b=sgu_b, merge_gain=merge_gain, w_out=w_out, ffn_w_up=ffn_w_up, ffn_conv=ffn_conv,
             ffn_w_down=ffn_w_down)
    return _forward(x, p)
```

```python
import functools
import math

import jax
import jax.numpy as jnp
from jax import lax
from jax.experimental import pallas as pl
from jax.experimental.pallas import tpu as pltpu

F32 = jnp.float32
BF16 = jnp.bfloat16

D_MODEL = 1024
GRID_W = 64
HEADS = 4
HEAD_DIM = 64
GROUP_W = HEADS * HEAD_DIM
DECAY_RANK = 16
ICLR_RANK = 16
GATE_RANK = 32
LORA_W = 128
DECAY_SCALE = math.exp(-0.5)
GN_EPS = 64e-5
NA_KH = 8
NA_KW = 16
SGU_CHUNK = 128
D_FF = 2816
NORM_EPS = 1e-6
NEG_BIG = -1e30

WKV_CHUNK = 64
WKV_BLOCK = 256
ROW_TILE = 512
NA_ROWS = 8
VMEM_LIMIT = 56 * 1024 * 1024


def _dot(a, b):
    return jnp.dot(a.astype(BF16), b.astype(BF16), preferred_element_type=F32)


def _split(x):
    hi = x.astype(BF16)
    lo = (x - hi.astype(F32)).astype(BF16)
    return hi, lo


def _dot_x3(a, b):
    ah, al = _split(a)
    bh, bl = _split(b)
    d = lambda p, q: jnp.dot(p, q, preferred_element_type=F32)
    return d(ah, bh) + (d(ah, bl) + d(al, bh))


def _sigmoid(x):
    return 1.0 / (1.0 + jnp.exp(-x))


def _rms(x, eps=NORM_EPS):
    return x * lax.rsqrt(jnp.mean(x * x, axis=-1, keepdims=True) + eps)


def _full(shape):
    n = len(shape)
    return pl.BlockSpec(shape, lambda *_: (0,) * n)


IN_GROUPS = (3 * GROUP_W, 3 * GROUP_W, 3 * GROUP_W, 2 * GROUP_W, LORA_W)
IN_COLS = sum(IN_GROUPS)


def _inproj_body(x_ref, g_ref, w_ref, conv_ref, rkv_ref, na_ref, sgu_ref, lora_ref):
    h = _rms(x_ref[...]) * g_ref[...]
    p = jnp.dot(h.astype(BF16), w_ref[...], preferred_element_type=F32)
    o = 0
    for ref, wdt in zip((conv_ref, rkv_ref, na_ref, sgu_ref, lora_ref), IN_GROUPS):
        ref[...] = p[:, o:o + wdt].astype(ref.dtype)
        o += wdt


def _inproj(x2, gain, w):
    n = x2.shape[0]
    tm = ROW_TILE
    row = lambda wdt: pl.BlockSpec((tm, wdt), lambda i: (i, 0))
    return pl.pallas_call(
        _inproj_body,
        grid=(n // tm,),
        in_specs=[row(D_MODEL), _full((1, D_MODEL)), _full((D_MODEL, IN_COLS))],
        out_specs=[row(w_) for w_ in IN_GROUPS],
        out_shape=[jax.ShapeDtypeStruct((n, IN_GROUPS[0]), BF16),
                   jax.ShapeDtypeStruct((n, IN_GROUPS[1]), BF16),
                   jax.ShapeDtypeStruct((n, IN_GROUPS[2]), BF16),
                   jax.ShapeDtypeStruct((n, IN_GROUPS[3]), BF16),
                   jax.ShapeDtypeStruct((n, IN_GROUPS[4]), F32)],
        compiler_params=pltpu.CompilerParams(dimension_semantics=("parallel",),
                                             vmem_limit_bytes=VMEM_LIMIT),
        name="inproj",
    )(x2, gain, w)


PAIR_W = 2 * HEAD_DIM
N_PAIRS = HEADS // 2
SOLVE_LEVELS = 6


def _mm(a, b):
    return jnp.dot(a, b, preferred_element_type=F32)


def _mm_nt(a, b):
    return lax.dot_general(a, b, (((1,), (1,)), ((), ())), preferred_element_type=F32)


def _mm_tn(a, b):
    return lax.dot_general(a, b, (((0,), (0,)), ((), ())), preferred_element_type=F32)


def _mm_x3(a, b):
    return _mm(a[0], b[0]) + (_mm(a[0], b[1]) + _mm(a[1], b[0]))


def _wkv_prep(rkv, lora, tri, seg, w0, wup, a0, aup, kk_gain, ka_gain, d):
    c = WKV_CHUNK
    nch = rkv.shape[0] // c
    r = rkv[:, 0:GROUP_W].astype(F32)
    k = rkv[:, GROUP_W:2 * GROUP_W].astype(F32)
    v = rkv[:, 2 * GROUP_W:3 * GROUP_W]

    kk = k * kk_gain
    sq_hi, sq_lo = _split(kk * kk)
    kk = kk * lax.rsqrt(_mm(sq_hi, seg) + _mm(sq_lo, seg) + 1e-12)
    lw = -DECAY_SCALE * _sigmoid(w0 + _dot_x3(jnp.tanh(lora), wup))
    a_lr = _sigmoid(a0 + _dot(lora, aup))
    k_eff = k * (1.0 + (a_lr - 1.0) * ka_gain)
    b_vec = kk * a_lr

    lw_hi, lw_lo = _split(lw)
    cum = _mm(tri, lw_hi) + _mm(tri, lw_lo)
    ends = [cum[ci * c + c - 1:ci * c + c] if d == 0 else cum[ci * c:ci * c + 1] for ci in range(nch)]
    tot = jnp.concatenate([jnp.broadcast_to(e, (c, GROUP_W)) for e in ends], axis=0)
    g_inv = jnp.exp(-cum)
    g_hat = jnp.exp(tot - cum)
    return dict(
        a_t=-kk * jnp.exp(cum - lw), r_t=r * jnp.exp(cum),
        b_t=(b_vec * g_inv).astype(BF16), k_t=(k_eff * g_inv).astype(BF16),
        b_h=(b_vec * g_hat).astype(BF16), k_h=(k_eff * g_hat).astype(BF16),
        v=v, g_end=[jnp.exp(e) for e in ends])


def _wkv_body(rkv_f, lora_f, rkv_b, lora_b, tri_ref, seg_ref, w0_ref, wup_ref, a0_ref, aup_ref,
              kk_ref, ka_ref, yf_ref, yb_ref, h_ref):
    @pl.when(pl.program_id(1) == 0)
    def _():
        h_ref[...] = jnp.zeros_like(h_ref)

    c = WKV_CHUNK
    nch = WKV_BLOCK // c
    seg = seg_ref[...]
    prep = [_wkv_prep(rkv_ref[...], lora_ref[...], tri_ref[d], seg, w0_ref[d], wup_ref[d], a0_ref[d],
                      aup_ref[d], kk_ref[...], ka_ref[...], d)
            for d, (rkv_ref, lora_ref) in enumerate(((rkv_f, lora_f), (rkv_b, lora_b)))]

    lane1 = lax.broadcasted_iota(jnp.int32, (1, PAIR_W), 1)
    head1 = [lane1 // HEAD_DIM == h for h in range(2)]
    row = lax.broadcasted_iota(jnp.int32, (PAIR_W, PAIR_W), 0)
    col = lax.broadcasted_iota(jnp.int32, (PAIR_W, PAIR_W), 1)
    t_idx, s_idx = row % c, col % c
    tri_mask = [(s_idx < t_idx) | ((row >= c) & (s_idx == t_idx)),
                (s_idx > t_idx) | ((row >= c) & (s_idx == t_idx))]
    diag = row == col
    own_block = (row // c) == (col // c)
    top_rows = lax.broadcasted_iota(jnp.int32, (PAIR_W, 1), 0) < c
    zeros_pair = jnp.zeros((c, PAIR_W), BF16)
    swap = lambda t: pltpu.roll(t, HEAD_DIM, axis=1)

    probs = [(d, ci, p) for d in range(2) for ci in range(nch) for p in range(N_PAIRS)]
    sl = lambda ci, p: (slice(ci * c, (ci + 1) * c), slice(p * PAIR_W, (p + 1) * PAIR_W))

    a4, v_sw = {}, {}
    for d, ci, p in probs:
        q = prep[d]
        rows, lanes = sl(ci, p)
        lhs = jnp.concatenate([q["a_t"][rows, lanes], q["r_t"][rows, lanes]], axis=0)
        rhs = jnp.concatenate([q["b_t"][rows, lanes], q["k_t"][rows, lanes]], axis=0)
        v_sw[d, ci, p] = swap(q["v"][rows, lanes].astype(F32)).astype(BF16)
        for h in range(2):
            m1 = _mm_nt(jnp.where(head1[h], lhs, 0.0).astype(BF16), rhs)
            a4[d, ci, p, h] = jnp.where(tri_mask[d], m1, 0.0)

    xs, s128 = {}, {}
    for d, ci, p in probs:
        q = prep[d]
        rows, lanes = sl(ci, p)
        v0 = jnp.concatenate([zeros_pair, v_sw[d, ci, p]], axis=0)
        for h in range(2):
            top = a4[d, ci, p, h][:c]
            xs[d, ci, p, h] = jnp.where(head1[h], q["a_t"][rows, lanes], _mm(top.astype(BF16), v0))
            s128[d, ci, p, h] = jnp.where(lane1 < c, top, 0.0)

    keys = [(d, ci, p, h) for d, ci, p in probs for h in range(2)]
    for j in range(SOLVE_LEVELS):
        for key in keys:
            s_op = s128[key][:, :c].astype(BF16)
            if j < SOLVE_LEVELS - 1:
                out = _mm(s_op, jnp.concatenate([s128[key], xs[key]], axis=1).astype(BF16))
                s128[key] = out[:, :PAIR_W]
                xs[key] = xs[key] + out[:, PAIR_W:]
            else:
                xs[key] = xs[key] + _mm(s_op, xs[key].astype(BF16))

    maps = {}
    for d, ci, p in probs:
        q = prep[d]
        rows, lanes = sl(ci, p)
        bk = jnp.concatenate([q["b_h"][rows, lanes], q["k_h"][rows, lanes]], axis=0)
        z = [jnp.concatenate([xs[d, ci, p, h].astype(BF16), jnp.where(head1[h], 0.0, v_sw[d, ci, p])], axis=0)
             for h in range(2)]
        ry = [_mm(a4[d, ci, p, h][c:].astype(BF16), z[h]) for h in range(2)]
        pq = _mm_tn(bk, jnp.concatenate(z, axis=1))
        rhat = q["r_t"][rows, lanes] + jnp.where(head1[0], ry[0], ry[1])
        yi = swap(jnp.where(head1[0], ry[1], ry[0]))
        pq = jnp.where(top_rows, pq[:, :PAIR_W], pq[:, PAIR_W:])
        pmat = jnp.where(own_block, pq, 0.0) + jnp.where(diag, q["g_end"][ci][:, lanes], 0.0)
        qmat = swap(jnp.where(own_block, 0.0, pq))
        maps[d, ci, p] = (jnp.concatenate([rhat, pmat], axis=0).astype(BF16), yi, qmat)

    for step in range(nch):
        for d, y_ref in enumerate((yf_ref, yb_ref)):
            ci = step if d == 0 else nch - 1 - step
            for p in range(N_PAIRS):
                rows, lanes = sl(ci, p)
                lhs, yi, qmat = maps[d, ci, p]
                out = _mm(lhs, jnp.concatenate(_split(h_ref[d, p]), axis=1))
                out = out[:, :PAIR_W] + out[:, PAIR_W:]
                y_ref[rows, lanes] = out[:c] + yi
                h_ref[d, p] = out[c:] + qmat


def _wkv(rkv, lora, batch, tri, seg, w0, wup, a0, aup, k_k, k_a):
    n = rkv.shape[0]
    nb = n // batch // WKV_BLOCK
    fwd = lambda wdt: pl.BlockSpec((WKV_BLOCK, wdt), lambda b, j: (b * nb + j, 0))
    bwd = lambda wdt: pl.BlockSpec((WKV_BLOCK, wdt), lambda b, j: (b * nb + nb - 1 - j, 0))
    return pl.pallas_call(
        _wkv_body,
        grid=(batch, nb),
        in_specs=[fwd(3 * GROUP_W), fwd(LORA_W), bwd(3 * GROUP_W), bwd(LORA_W),
                  _full(tri.shape), _full(seg.shape), _full(w0.shape), _full(wup.shape),
                  _full(a0.shape), _full(aup.shape), _full(k_k.shape), _full(k_a.shape)],
        out_specs=[fwd(GROUP_W), bwd(GROUP_W)],
        out_shape=[jax.ShapeDtypeStruct((n, GROUP_W), F32)] * 2,
        scratch_shapes=[pltpu.VMEM((2, N_PAIRS, PAIR_W, PAIR_W), F32)],
        compiler_params=pltpu.CompilerParams(dimension_semantics=("parallel", "arbitrary"),
                                             vmem_limit_bytes=VMEM_LIMIT),
        name="wkv7",
    )(rkv, lora, rkv, lora, tri, seg, w0, wup, a0, aup, k_k, k_a)


def _na_body(q_ref, k_ref, v_ref, bias_ref, o_ref, *, rows):
    j = pl.program_id(1)
    band = NA_KH * GRID_W
    lane = lax.broadcasted_iota(jnp.int32, (1, GROUP_W), 1) // HEAD_DIM
    starts, scores = [], []
    for rr in range(NA_ROWS):
        r = j * NA_ROWS + rr
        rs = jnp.clip(r - NA_KH // 2, 0, rows - NA_KH)
        start = pl.multiple_of(rs * GRID_W, GRID_W)
        q = q_ref[rr * GRID_W:(rr + 1) * GRID_W, :] * (HEAD_DIM ** -0.5)
        qs = jnp.concatenate([jnp.where(lane == h, q, 0.0) for h in range(HEADS)], axis=0).astype(BF16)
        s = _mm_nt(qs, k_ref[pl.ds(start, band), :])
        starts.append(start)
        scores.append(s + bias_ref[rs - r + (NA_KH - 1)])
    probs, norms = [], []
    for s in scores:
        p = jnp.exp(s - jnp.max(s, axis=-1, keepdims=True))
        norms.append(jnp.sum(p, axis=-1, keepdims=True))
        probs.append(p.astype(BF16))
    for rr in range(NA_ROWS):
        o = _mm(probs[rr], v_ref[pl.ds(starts[rr], band), :]) / norms[rr]
        acc = o[0:GRID_W]
        for h in range(1, HEADS):
            acc = jnp.where(lane == h, o[h * GRID_W:(h + 1) * GRID_W], acc)
        o_ref[rr * GRID_W:(rr + 1) * GRID_W, :] = acc.astype(o_ref.dtype)


def _na(na, bias, batch):
    n = na.shape[0]
    t = n // batch
    rows = t // GRID_W
    nj = rows // NA_ROWS
    blk = NA_ROWS * GRID_W
    return pl.pallas_call(
        functools.partial(_na_body, rows=rows),
        grid=(batch, nj),
        in_specs=[pl.BlockSpec((blk, GROUP_W), lambda b, j: (b * nj + j, 0)),
                  pl.BlockSpec((t, GROUP_W), lambda b, j: (b, 1)),
                  pl.BlockSpec((t, GROUP_W), lambda b, j: (b, 2)),
                  _full(bias.shape)],
        out_specs=pl.BlockSpec((blk, GROUP_W), lambda b, j: (b * nj + j, 0)),
        out_shape=jax.ShapeDtypeStruct((n, GROUP_W), BF16),
        compiler_params=pltpu.CompilerParams(dimension_semantics=("parallel", "arbitrary"),
                                             vmem_limit_bytes=VMEM_LIMIT),
        name="nattn",
    )(na, na, na, bias)


def _na_bias_table(rpb, rows):
    kh = min(NA_KH, rows)
    c = jnp.arange(GRID_W)
    col_start = jnp.clip(c - NA_KW // 2, 0, GRID_W - NA_KW)
    col_mask = (c[None, :] >= col_start[:, None]) & (c[None, :] < col_start[:, None] + NA_KW)
    dx = jnp.clip(c[None, :] - c[:, None], -(NA_KW - 1), NA_KW - 1) + (NA_KW - 1)
    onehot = (dx[None] == jnp.arange(2 * NA_KW - 1)[:, None, None]).astype(F32)
    by_dy = jnp.einsum("hyx,xqw->hyqw", rpb, onehot, precision=lax.Precision.HIGHEST)
    by_dy = jnp.where(col_mask[None, None], by_dy, NEG_BIG)
    b = jnp.stack([by_dy[:, delta:delta + kh] for delta in range(NA_KH)], axis=0)
    b = jnp.transpose(b, (0, 1, 3, 2, 4))
    return b.reshape(NA_KH, HEADS * GRID_W, kh * GRID_W).astype(F32)


def _seg_sum(x, seg):
    hi, lo = _split(x)
    return _mm(hi, seg) + _mm(lo, seg)


def _seg_mean(x, seg):
    return _seg_sum(x, seg) * (1.0 / HEAD_DIM)


def _merge_body(x_ref, conv_ref, cprev_ref, cnext_ref, rkv_ref, lora_ref, yf_ref, yb_ref, na_ref, sgu_ref,
                seg_ref, convw_ref, gup_ref, rk_ref, lnw_ref, lnb_ref, sgn_ref, sgw_ref, sgb_ref,
                mg_ref, wout_ref, gpost_ref, o_ref, *, tiles_per_seq):
    tm = ROW_TILE
    i = pl.program_id(0)
    seg = seg_ref[...]

    conv = conv_ref[...].astype(F32)
    z = conv[:, 2 * GROUP_W:] * conv[:, :GROUP_W]
    zp = cprev_ref[15:16, :].astype(F32)
    zp = zp[:, 2 * GROUP_W:] * zp[:, :GROUP_W]
    zn = cnext_ref[0:1, :].astype(F32)
    zn = zn[:, 2 * GROUP_W:] * zn[:, :GROUP_W]
    first = (i % tiles_per_seq) == 0
    last = (i % tiles_per_seq) == tiles_per_seq - 1
    zp = jnp.where(first, 0.0, zp)
    zn = jnp.where(last, 0.0, zn)
    ridx = lax.broadcasted_iota(jnp.int32, (tm, 1), 0)
    z_prev = jnp.where(ridx == 0, zp, pltpu.roll(z, 1, axis=0))
    z_next = jnp.where(ridx == tm - 1, zn, pltpu.roll(z, tm - 1, axis=0))
    cw = convw_ref[...]
    y_conv = conv[:, GROUP_W:2 * GROUP_W] * (z_prev * cw[0:1] + z * cw[1:2] + z_next * cw[2:3])

    rkv = rkv_ref[...].astype(F32)
    r, k, v = rkv[:, :GROUP_W], rkv[:, GROUP_W:2 * GROUP_W], rkv[:, 2 * GROUP_W:]
    y = yf_ref[...] + yb_ref[...]
    mu = _seg_mean(y, seg)
    yc = y - mu
    var = _seg_mean(yc * yc, seg)
    yn = yc * lax.rsqrt(var + GN_EPS) * lnw_ref[...] + lnb_ref[...]
    bonus = _seg_sum(r * k * rk_ref[...], seg) * v
    gate = _dot(_sigmoid(lora_ref[...]), gup_ref[...])
    y_rwkv = (yn + bonus) * gate

    sg = sgu_ref[...].astype(F32)
    u = jax.nn.gelu(sg[:, :GROUP_W])
    gv = jax.nn.gelu(sg[:, GROUP_W:])
    gmu = jnp.mean(gv, axis=-1, keepdims=True)
    gc = gv - gmu
    gvn = gc * lax.rsqrt(jnp.mean(gc * gc, axis=-1, keepdims=True) + NORM_EPS) * sgn_ref[...]
    lane = lax.broadcasted_iota(jnp.int32, (1, GROUP_W), 1) // HEAD_DIM
    sgw = sgw_ref[...]
    mixed = []
    for ci in range(tm // SGU_CHUNK):
        res = _dot(sgw, gvn[ci * SGU_CHUNK:(ci + 1) * SGU_CHUNK])
        m = res[0:SGU_CHUNK]
        for h in range(1, HEADS):
            m = jnp.where(lane == h, res[h * SGU_CHUNK:(h + 1) * SGU_CHUNK], m)
        mixed.append(m + sgb_ref[...])
    y_sgu = u * jnp.concatenate(mixed, axis=0)

    y_na = na_ref[...].astype(F32)
    merged = jnp.concatenate([_rms(y_conv), _rms(y_rwkv), _rms(y_na), _rms(y_sgu)], axis=1) * mg_ref[...]
    out = jnp.dot(merged.astype(BF16), wout_ref[...], preferred_element_type=F32)
    o_ref[...] = x_ref[...] + _rms(out) * gpost_ref[...]


def _merge(x2, conv, rkv, lora, yf, yb, y_na, sgu, batch, seg, convw, gup, rk, lnw, lnb, sgn, sgw, sgb,
           mg, wout, gpost):
    n = x2.shape[0]
    tm = ROW_TILE
    tiles_per_seq = n // batch // tm
    nh = n // 16
    row = lambda wdt: pl.BlockSpec((tm, wdt), lambda i: (i, 0))
    prev = pl.BlockSpec((16, 3 * GROUP_W), lambda i: (jnp.maximum(i * (tm // 16) - 1, 0), 0))
    nxt = pl.BlockSpec((16, 3 * GROUP_W), lambda i: (jnp.minimum((i + 1) * (tm // 16), nh - 1), 0))
    params = (seg, convw, gup, rk, lnw, lnb, sgn, sgw, sgb, mg, wout, gpost)
    return pl.pallas_call(
        functools.partial(_merge_body, tiles_per_seq=tiles_per_seq),
        grid=(n // tm,),
        in_specs=[row(D_MODEL), row(3 * GROUP_W), prev, nxt, row(3 * GROUP_W), row(LORA_W),
                  row(GROUP_W), row(GROUP_W), row(GROUP_W), row(2 * GROUP_W)]
                 + [_full(p.shape) for p in params],
        out_specs=row(D_MODEL),
        out_shape=jax.ShapeDtypeStruct((n, D_MODEL), F32),
        compiler_params=pltpu.CompilerParams(dimension_semantics=("parallel",),
                                             vmem_limit_bytes=VMEM_LIMIT),
        name="merge",
    )(x2, conv, conv, conv, rkv, lora, yf, yb, y_na, sgu, *params)


FFN_SPLIT = 2
FFN_SLAB = D_FF // FFN_SPLIT
HALO = 8


def _ffn_body(x_ref, xprev_ref, xnext_ref, gpre_ref, wup_ref, cw_ref, wdn_ref, gpost_ref, o_ref, *,
              tiles_per_seq):
    tm = ROW_TILE
    i = pl.program_id(0)
    first = (i % tiles_per_seq) == 0
    last = (i % tiles_per_seq) == tiles_per_seq - 1
    x = x_ref[...]
    g = gpre_ref[...]
    hf = _rms(x) * g
    hp = jnp.where(first, 0.0, _rms(xprev_ref[...]) * g)
    hn = jnp.where(last, 0.0, _rms(xnext_ref[...]) * g)
    h = hf.astype(BF16)
    hext = jnp.concatenate([hp, hf, hn], axis=0).astype(BF16)
    ext = tm + 2 * HALO
    acc = jnp.zeros((tm, D_MODEL), F32)
    for s in range(FFN_SPLIT):
        c0 = s * FFN_SLAB
        gate = jnp.dot(hext, wup_ref[:, c0:c0 + FFN_SLAB], preferred_element_type=F32)
        lin = jnp.dot(h, wup_ref[:, D_FF + c0:D_FF + c0 + FFN_SLAB], preferred_element_type=F32)
        cw = cw_ref[:, c0:c0 + FFN_SLAB]
        g_prev = pltpu.roll(gate, 1, axis=0)[HALO:HALO + tm]
        g_next = pltpu.roll(gate, ext - 1, axis=0)[HALO:HALO + tm]
        cv = g_prev * cw[0:1] + gate[HALO:HALO + tm] * cw[1:2] + g_next * cw[2:3]
        hid = jax.nn.gelu(cv) * lin
        acc = acc + jnp.dot(hid.astype(BF16), wdn_ref[c0:c0 + FFN_SLAB, :], preferred_element_type=F32)
    o_ref[...] = x + _rms(acc) * gpost_ref[...]


def _ffn(x2, batch, gpre, wup, cw, wdn, gpost):
    n = x2.shape[0]
    tm = ROW_TILE
    tiles_per_seq = n // batch // tm
    nh = n // HALO
    row = pl.BlockSpec((tm, D_MODEL), lambda i: (i, 0))
    prev = pl.BlockSpec((HALO, D_MODEL), lambda i: (jnp.maximum(i * (tm // HALO) - 1, 0), 0))
    nxt = pl.BlockSpec((HALO, D_MODEL), lambda i: (jnp.minimum((i + 1) * (tm // HALO), nh - 1), 0))
    params = (gpre, wup, cw, wdn, gpost)
    return pl.pallas_call(
        functools.partial(_ffn_body, tiles_per_seq=tiles_per_seq),
        grid=(n // tm,),
        in_specs=[row, prev, nxt] + [_full(p.shape) for p in params],
        out_specs=row,
        out_shape=jax.ShapeDtypeStruct((n, D_MODEL), F32),
        compiler_params=pltpu.CompilerParams(dimension_semantics=("parallel",),
                                             vmem_limit_bytes=VMEM_LIMIT),
        name="convffn",
    )(x2, x2, x2, *params)


def _regroup_w_in(w):
    g = GROUP_W
    lora0 = 6 * g
    lora1 = lora0 + 2 * DECAY_RANK + 2 * ICLR_RANK + GATE_RANK
    pad = jnp.zeros((w.shape[0], LORA_W - (lora1 - lora0)), w.dtype)
    return jnp.concatenate([w[:, :lora0], w[:, lora1:lora1 + 5 * g], w[:, lora0:lora1], pad], axis=1).astype(BF16)


def _pad_rows(w, start):
    return jnp.zeros((LORA_W, w.shape[-1]), w.dtype).at[start:start + w.shape[0]].set(w)


def _layer_params(l, p, rows):
    wup = jnp.stack([_pad_rows(p["rwkv_w_up"][l, d], d * DECAY_RANK) for d in range(2)])
    aup = jnp.stack([_pad_rows(p["rwkv_a_up"][l, d], 2 * DECAY_RANK + d * ICLR_RANK)
                     for d in range(2)]).astype(BF16)
    gup = _pad_rows(p["rwkv_g_up"][l], 2 * DECAY_RANK + 2 * ICLR_RANK).astype(BF16)
    row2 = lambda a: a.reshape(1, -1)
    sgb = jnp.repeat(p["sgu_b"][l].T, HEAD_DIM, axis=1)
    return dict(
        w_in=_regroup_w_in(p["w_in"][l]), g_mix_pre=row2(p["norm_mix_pre"][l]),
        w0=p["rwkv_w0"][l].reshape(2, 1, GROUP_W), wup=wup, a0=p["rwkv_a0"][l].reshape(2, 1, GROUP_W), aup=aup,
        k_k=row2(p["rwkv_k_k"][l]), k_a=row2(p["rwkv_k_a"][l]),
        bias=_na_bias_table(p["na_rpb"][l], rows),
        convw=p["conv_a_w"][l], gup=gup, rk=row2(p["rwkv_r_k"][l]), lnw=row2(p["rwkv_lnx_w"][l]),
        lnb=row2(p["rwkv_lnx_b"][l]), sgn=row2(p["sgu_norm"][l]),
        sgw=p["sgu_w"][l].reshape(HEADS * SGU_CHUNK, SGU_CHUNK).astype(BF16), sgb=sgb,
        mg=row2(p["merge_gain"][l]), wout=p["w_out"][l].astype(BF16), g_mix_post=row2(p["norm_mix_post"][l]),
        g_ffn_pre=row2(p["norm_ffn_pre"][l]), ffn_up=p["ffn_w_up"][l].astype(BF16), ffn_conv=p["ffn_conv"][l],
        ffn_down=p["ffn_w_down"][l].astype(BF16), g_ffn_post=row2(p["norm_ffn_post"][l]),
    )


def _constants():
    i = jnp.arange(WKV_BLOCK)
    same = (i[None, :] // WKV_CHUNK) == (i[:, None] // WKV_CHUNK)
    tri = jnp.stack([same & (i[None, :] <= i[:, None]),
                     same & (i[None, :] >= i[:, None])]).astype(BF16)
    g = jnp.arange(GROUP_W) // HEAD_DIM
    seg = (g[:, None] == g[None, :]).astype(BF16)
    return tri, seg


def _forward(x, p):
    batch, t, _ = x.shape
    assert t % ROW_TILE == 0 and t % (NA_ROWS * GRID_W) == 0 and t // GRID_W >= NA_KH
    x2 = x.reshape(batch * t, D_MODEL)
    tri, seg = _constants()
    for l in range(p["w_in"].shape[0]):
        q = _layer_params(l, p, t // GRID_W)
        conv, rkv, na, sgu, lora = _inproj(x2, q["g_mix_pre"], q["w_in"])
        yf, yb = _wkv(rkv, lora, batch, tri, seg, q["w0"], q["wup"], q["a0"], q["aup"], q["k_k"], q["k_a"])
        y_na = _na(na, q["bias"], batch)
        x2 = _merge(x2, conv, rkv, lora, yf, yb, y_na, sgu, batch, seg, q["convw"], q["gup"], q["rk"],
                    q["lnw"], q["lnb"], q["sgn"], q["sgw"], q["sgb"], q["mg"], q["wout"], q["g_mix_post"])
        x2 = _ffn(x2, batch, q["g_ffn_pre"], q["ffn_up"], q["ffn_conv"], q["ffn_down"], q["g_ffn_post"])
    return x2.reshape(batch, t, D_MODEL)


def kernel(x, norm_mix_pre, norm_mix_post, norm_ffn_pre, norm_ffn_post, w_in, conv_a_w, rwkv_w0, rwkv_w_up,
           rwkv_a0, rwkv_a_up, rwkv_g_up, rwkv_k_k, rwkv_k_a, rwkv_r_k, rwkv_lnx_w, rwkv_lnx_b, na_rpb,
           sgu_norm, sgu_w, sgu_b, merge_gain, w_out, ffn_w_up, ffn_conv, ffn_w_down):
    p = dict(norm_mix_pre=norm_mix_pre, norm_mix_post=norm_mix_post, norm_ffn_pre=norm_ffn_pre,
             norm_ffn_post=norm_ffn_post, w_in=w_in, conv_a_w=conv_a_w, rwkv_w0=rwkv_w0, rwkv_w_up=rwkv_w_up,
             rwkv_a0=rwkv_a0, rwkv_a_up=rwkv_a_up, rwkv_g_up=rwkv_g_up, rwkv_k_k=rwkv_k_k, rwkv_k_a=rwkv_k_a,
             rwkv_r_k=rwkv_r_k, rwkv_lnx_w=rwkv_lnx_w, rwkv_lnx_b=rwkv_lnx_b, na_rpb=na_rpb, sgu_norm=sgu_norm,
             sgu_w=sgu_w, sgu_b=sgu_b, merge_gain=merge_gain, w_out=w_out, ffn_w_up=ffn_w_up, ffn_conv=ffn_conv,
             ffn_w_down=ffn_w_down)
    return _forward(x, p)
```

```python
import functools
import math

import jax
import jax.numpy as jnp
from jax import lax
from jax.experimental import pallas as pl
from jax.experimental.pallas import tpu as pltpu

F32 = jnp.float32
BF16 = jnp.bfloat16

D_MODEL = 1024
GRID_W = 64
HEADS = 4
HEAD_DIM = 64
GROUP_W = HEADS * HEAD_DIM
DECAY_RANK = 16
ICLR_RANK = 16
GATE_RANK = 32
LORA_W = 128
DECAY_SCALE = math.exp(-0.5)
GN_EPS = 64e-5
NA_KH = 8
NA_KW = 16
SGU_CHUNK = 128
D_FF = 2816
NORM_EPS = 1e-6
NEG_BIG = -1e30

WKV_CHUNK = 64
WKV_BLOCK = 256
ROW_TILE = 512
NA_ROWS = 8
VMEM_LIMIT = 56 * 1024 * 1024


def _dot(a, b):
    return jnp.dot(a.astype(BF16), b.astype(BF16), preferred_element_type=F32)


def _split(x):
    hi = x.astype(BF16)
    lo = (x - hi.astype(F32)).astype(BF16)
    return hi, lo


def _dot_x3(a, b):
    ah, al = _split(a)
    bh, bl = _split(b)
    d = lambda p, q: jnp.dot(p, q, preferred_element_type=F32)
    return d(ah, bh) + (d(ah, bl) + d(al, bh))


def _sigmoid(x):
    return 1.0 / (1.0 + jnp.exp(-x))


def _rms(x, eps=NORM_EPS):
    return x * lax.rsqrt(jnp.mean(x * x, axis=-1, keepdims=True) + eps)


def _full(shape):
    n = len(shape)
    return pl.BlockSpec(shape, lambda *_: (0,) * n)


IN_GROUPS = (3 * GROUP_W, 3 * GROUP_W, 3 * GROUP_W, 2 * GROUP_W, LORA_W)
IN_COLS = sum(IN_GROUPS)
RKVK_W = 4 * GROUP_W


def _inproj_body(x_ref, g_ref, w_ref, seg_ref, kk_ref, w0_ref, wup_ref, a0_ref, aup_ref,
                 conv_ref, rkvk_ref, na_ref, sgu_ref, lora_ref, lw_ref, nalr_ref):
    half = ROW_TILE // 2
    seg = seg_ref[...]
    for r0 in (0, half):
        rows = slice(r0, r0 + half)
        h = _rms(x_ref[rows, :]) * g_ref[...]
        p = jnp.dot(h.astype(BF16), w_ref[...], preferred_element_type=F32)
        o = 0
        for ref, wdt in zip((conv_ref, rkvk_ref, na_ref, sgu_ref, lora_ref), IN_GROUPS):
            ref[rows, 0:wdt] = p[:, o:o + wdt].astype(ref.dtype)
            o += wdt
        k = p[:, IN_GROUPS[0] + GROUP_W:IN_GROUPS[0] + 2 * GROUP_W]
        lora = p[:, IN_COLS - LORA_W:]
        kk = k * kk_ref[...]
        rkvk_ref[rows, 3 * GROUP_W:] = (kk * lax.rsqrt(_seg_sum(kk * kk, seg) + 1e-12)).astype(BF16)
        th = _split(jnp.tanh(lora))
        lora_bf = lora.astype(BF16)
        for d in range(2):
            lanes = slice(d * GROUP_W, (d + 1) * GROUP_W)
            lw_ref[rows, lanes] = -DECAY_SCALE * _sigmoid(w0_ref[d] + _mm_x3(th, (wup_ref[0, d], wup_ref[1, d])))
            nalr_ref[rows, lanes] = (-_sigmoid(a0_ref[d] + _mm(lora_bf, aup_ref[d]))).astype(BF16)


def _inproj(x2, gain, w, seg, k_k, w0, wup, a0, aup):
    n = x2.shape[0]
    tm = ROW_TILE
    row = lambda wdt: pl.BlockSpec((tm, wdt), lambda i: (i, 0))
    params = (gain, w, seg, k_k, w0, wup, a0, aup)
    widths = (IN_GROUPS[0], RKVK_W, IN_GROUPS[2], IN_GROUPS[3], IN_GROUPS[4], 2 * GROUP_W, 2 * GROUP_W)
    dtypes = (BF16, BF16, BF16, BF16, F32, F32, BF16)
    return pl.pallas_call(
        _inproj_body,
        grid=(n // tm,),
        in_specs=[row(D_MODEL)] + [_full(q.shape) for q in params],
        out_specs=[row(w_) for w_ in widths],
        out_shape=[jax.ShapeDtypeStruct((n, w_), dt) for w_, dt in zip(widths, dtypes)],
        compiler_params=pltpu.CompilerParams(dimension_semantics=("parallel",),
                                             vmem_limit_bytes=VMEM_LIMIT),
        name="inproj",
    )(x2, *params)


PAIR_W = 2 * HEAD_DIM
N_PAIRS = HEADS // 2
SOLVE_LEVELS = 6


def _mm(a, b):
    return jnp.dot(a, b, preferred_element_type=F32)


def _mm_nt(a, b):
    return lax.dot_general(a, b, (((1,), (1,)), ((), ())), preferred_element_type=F32)


def _mm_tn(a, b):
    return lax.dot_general(a, b, (((0,), (0,)), ((), ())), preferred_element_type=F32)


def _mm_x3(a, b):
    return _mm(a[0], b[0]) + (_mm(a[0], b[1]) + _mm(a[1], b[0]))


def _wkv_prep(rkvk, lw, nalr, tri, ka_gain, d):
    c = WKV_CHUNK
    nch = rkvk.shape[0] // c
    r = rkvk[:, 0:GROUP_W].astype(F32)
    k = rkvk[:, GROUP_W:2 * GROUP_W].astype(F32)
    v = rkvk[:, 2 * GROUP_W:3 * GROUP_W]
    kk = rkvk[:, 3 * GROUP_W:].astype(F32)
    nalr = nalr.astype(F32)
    k_eff = k * (1.0 - (1.0 + nalr) * ka_gain)
    b_vec = kk * nalr

    lw_hi, lw_lo = _split(lw)
    cum = _mm(tri, lw_hi) + _mm(tri, lw_lo)
    ends = [cum[ci * c + c - 1:ci * c + c] if d == 0 else cum[ci * c:ci * c + 1] for ci in range(nch)]
    g_end = [jnp.exp(e) for e in ends]
    g_inc = jnp.exp(cum)
    g_inv = 1.0 / g_inc
    g_hat = jnp.concatenate([jnp.broadcast_to(e, (c, GROUP_W)) for e in g_end], axis=0) * g_inv
    return dict(
        a_t=kk * jnp.exp(cum - lw), r_t=r * g_inc,
        b_t=(b_vec * g_inv).astype(BF16), k_t=(k_eff * g_inv).astype(BF16),
        b_h=(b_vec * g_hat).astype(BF16), k_h=(k_eff * g_hat).astype(BF16),
        v=v, g_end=g_end)


def _wkv_body(rkvk_f, lw_f, nalr_f, rkvk_b, lw_b, nalr_b, tri_ref, ka_ref, yf_ref, yb_ref, h_ref):
    @pl.when(pl.program_id(1) == 0)
    def _():
        h_ref[...] = jnp.zeros_like(h_ref)

    c = WKV_CHUNK
    nch = WKV_BLOCK // c
    in_refs = ((rkvk_f, lw_f, nalr_f), (rkvk_b, lw_b, nalr_b))
    y_refs = (yf_ref, yb_ref)

    def head_lanes(pairs_wide):
        lane = lax.broadcasted_iota(jnp.int32, (1, pairs_wide * PAIR_W), 1)
        return [(lane % PAIR_W) // HEAD_DIM == h for h in range(2)]

    head1, head2, head3 = head_lanes(1), head_lanes(2), head_lanes(3)
    row = lax.broadcasted_iota(jnp.int32, (PAIR_W, PAIR_W), 0)
    col = lax.broadcasted_iota(jnp.int32, (PAIR_W, PAIR_W), 1)
    t_idx, s_idx = row % c, col % c
    tri_mask = [(s_idx < t_idx) | ((row >= c) & (s_idx == t_idx)),
                (s_idx > t_idx) | ((row >= c) & (s_idx == t_idx))]
    diag = row == col
    own_block = (row // c) == (col // c)
    zeros_pair = jnp.zeros((c, PAIR_W), BF16)

    cps = [(ci, p) for ci in range(nch) for p in range(N_PAIRS)]
    sl = lambda ci, p: (slice(ci * c, (ci + 1) * c), slice(p * PAIR_W, (p + 1) * PAIR_W))
    st = [dict(a4={}, xa={}, xv={}, s_bf={}, maps={}) for _ in range(2)]

    def prep_stage(d):
        rkvk_ref, lw_ref, nalr_ref = in_refs[d]
        st[d]["q"] = _wkv_prep(rkvk_ref[...], lw_ref[...], nalr_ref[...], tri_ref[d], ka_ref[...], d)

    def score_stage(d, todo):
        q = st[d]["q"]
        for ci, p in todo:
            rows, lanes = sl(ci, p)
            lhs = jnp.concatenate([q["a_t"][rows, lanes], q["r_t"][rows, lanes]], axis=0).astype(BF16)
            b_t, k_t = q["b_t"][rows, lanes], q["k_t"][rows, lanes]
            for h, rhs in enumerate((jnp.concatenate([b_t, k_t], axis=0), jnp.concatenate([k_t, b_t], axis=0))):
                m1 = _mm_nt(jnp.where(head1[h], lhs, 0.0), rhs)
                st[d]["a4"][ci, p, h] = jnp.where(tri_mask[d], m1, 0.0).astype(BF16)

    def rhs_stage(d, todo):
        q = st[d]["q"]
        for ci, p in todo:
            rows, lanes = sl(ci, p)
            v = q["v"][rows, lanes]
            top = [st[d]["a4"][ci, p, h][:c] for h in range(2)]
            akv = [_mm(top[0], jnp.concatenate([zeros_pair, v], axis=0)),
                   _mm(top[1], jnp.concatenate([v, zeros_pair], axis=0))]
            st[d]["xa"][ci, p] = q["a_t"][rows, lanes]
            st[d]["xv"][ci, p] = jnp.where(head1[0], akv[0], akv[1])
            st[d]["s_bf"][ci, p] = jnp.where(head1[0], top[0], top[1])

    def solve_level(d, j):
        xa, xv, s_bf = st[d]["xa"], st[d]["xv"], st[d]["s_bf"]
        last = j == SOLVE_LEVELS - 1
        heads = head2 if last else head3
        for key in cps:
            parts = ([] if last else [s_bf[key]]) + [xa[key].astype(BF16), xv[key].astype(BF16)]
            cat = jnp.concatenate(parts, axis=1)
            out = _mm(s_bf[key], jnp.concatenate([jnp.where(heads[0], cat, 0.0), jnp.where(heads[1], cat, 0.0)],
                                                 axis=0))
            o = 0
            if not last:
                s_bf[key] = out[:, :PAIR_W].astype(BF16)
                o = PAIR_W
            xa[key] = xa[key] + out[:, o:o + PAIR_W]
            xv[key] = xv[key] + out[:, o + PAIR_W:]

    def map_stage(d, todo):
        q = st[d]["q"]
        for ci, p in todo:
            rows, lanes = sl(ci, p)
            u = jnp.concatenate([st[d]["xa"][ci, p].astype(BF16), st[d]["xv"][ci, p].astype(BF16)], axis=1)
            vz = jnp.concatenate([zeros_pair, q["v"][rows, lanes]], axis=1)
            uv = jnp.concatenate([u, vz], axis=0)
            ry = jnp.where(head2[0], _mm(st[d]["a4"][ci, p, 0][c:], uv),
                           _mm(st[d]["a4"][ci, p, 1][c:], jnp.concatenate([vz, u], axis=0)))
            pq = _mm_tn(jnp.concatenate([q["b_h"][rows, lanes], q["k_h"][rows, lanes]], axis=0), uv)
            rhat = q["r_t"][rows, lanes] + ry[:, :PAIR_W]
            pmat = jnp.where(own_block, pq[:, :PAIR_W], 0.0) + jnp.where(diag, q["g_end"][ci][:, lanes], 0.0)
            qmat = jnp.where(own_block, pq[:, PAIR_W:], 0.0)
            st[d]["maps"][ci, p] = (jnp.concatenate([rhat, pmat], axis=0).astype(BF16), ry[:, PAIR_W:], qmat)

    def carry_step(d, step):
        ci = step if d == 0 else nch - 1 - step
        for p in range(N_PAIRS):
            rows, lanes = sl(ci, p)
            lhs, yi, qmat = st[d]["maps"][ci, p]
            out = _mm(lhs, jnp.concatenate(_split(h_ref[d, p]), axis=1))
            out = out[:, :PAIR_W] + out[:, PAIR_W:]
            y_refs[d][rows, lanes] = out[:c] + yi
            h_ref[d, p] = out[c:] + qmat

    for stage in (prep_stage, lambda d: score_stage(d, cps), lambda d: rhs_stage(d, cps)):
        for d in range(2):
            stage(d)
    for j in range(SOLVE_LEVELS):
        for d in range(2):
            solve_level(d, j)
    for d in range(2):
        map_stage(d, cps)
    for step in range(nch):
        for d in range(2):
            carry_step(d, step)


def _wkv(rkvk, lw, nalr, batch, tri, k_a):
    n = rkvk.shape[0]
    nb = n // batch // WKV_BLOCK
    fwd = lambda wdt, col=0: pl.BlockSpec((WKV_BLOCK, wdt), lambda b, j: (b * nb + j, col))
    bwd = lambda wdt, col=0: pl.BlockSpec((WKV_BLOCK, wdt), lambda b, j: (b * nb + nb - 1 - j, col))
    return pl.pallas_call(
        _wkv_body,
        grid=(batch, nb),
        in_specs=[fwd(RKVK_W), fwd(GROUP_W, 0), fwd(GROUP_W, 0), bwd(RKVK_W), bwd(GROUP_W, 1), bwd(GROUP_W, 1),
                  _full(tri.shape), _full(k_a.shape)],
        out_specs=[fwd(GROUP_W), bwd(GROUP_W)],
        out_shape=[jax.ShapeDtypeStruct((n, GROUP_W), F32)] * 2,
        scratch_shapes=[pltpu.VMEM((2, N_PAIRS, PAIR_W, PAIR_W), F32)],
        compiler_params=pltpu.CompilerParams(dimension_semantics=("parallel", "arbitrary"),
                                             vmem_limit_bytes=VMEM_LIMIT),
        name="wkv7",
    )(rkvk, lw, nalr, rkvk, lw, nalr, tri, k_a)


def _na_body(q_ref, k_ref, v_ref, bias_ref, o_ref, *, rows):
    j = pl.program_id(1)
    band = NA_KH * GRID_W
    lane = lax.broadcasted_iota(jnp.int32, (1, GROUP_W), 1) // HEAD_DIM
    starts, scores = [], []
    for rr in range(NA_ROWS):
        r = j * NA_ROWS + rr
        rs = jnp.clip(r - NA_KH // 2, 0, rows - NA_KH)
        start = pl.multiple_of(rs * GRID_W, GRID_W)
        q = q_ref[rr * GRID_W:(rr + 1) * GRID_W, :] * (HEAD_DIM ** -0.5)
        qs = jnp.concatenate([jnp.where(lane == h, q, 0.0) for h in range(HEADS)], axis=0).astype(BF16)
        s = _mm_nt(qs, k_ref[pl.ds(start, band), :])
        starts.append(start)
        scores.append(s + bias_ref[rs - r + (NA_KH - 1)])
    probs, norms = [], []
    for s in scores:
        p = jnp.exp(s - jnp.max(s, axis=-1, keepdims=True))
        norms.append(jnp.sum(p, axis=-1, keepdims=True))
        probs.append(p.astype(BF16))
    for rr in range(NA_ROWS):
        o = _mm(probs[rr], v_ref[pl.ds(starts[rr], band), :]) / norms[rr]
        acc = o[0:GRID_W]
        for h in range(1, HEADS):
            acc = jnp.where(lane == h, o[h * GRID_W:(h + 1) * GRID_W], acc)
        o_ref[rr * GRID_W:(rr + 1) * GRID_W, :] = acc.astype(o_ref.dtype)


def _na(na, bias, batch):
    n = na.shape[0]
    t = n // batch
    rows = t // GRID_W
    nj = rows // NA_ROWS
    blk = NA_ROWS * GRID_W
    return pl.pallas_call(
        functools.partial(_na_body, rows=rows),
        grid=(batch, nj),
        in_specs=[pl.BlockSpec((blk, GROUP_W), lambda b, j: (b * nj + j, 0)),
                  pl.BlockSpec((t, GROUP_W), lambda b, j: (b, 1)),
                  pl.BlockSpec((t, GROUP_W), lambda b, j: (b, 2)),
                  _full(bias.shape)],
        out_specs=pl.BlockSpec((blk, GROUP_W), lambda b, j: (b * nj + j, 0)),
        out_shape=jax.ShapeDtypeStruct((n, GROUP_W), BF16),
        compiler_params=pltpu.CompilerParams(dimension_semantics=("parallel", "arbitrary"),
                                             vmem_limit_bytes=VMEM_LIMIT),
        name="nattn",
    )(na, na, na, bias)


def _na_bias_table(rpb, rows):
    kh = min(NA_KH, rows)
    c = jnp.arange(GRID_W)
    col_start = jnp.clip(c - NA_KW // 2, 0, GRID_W - NA_KW)
    col_mask = (c[None, :] >= col_start[:, None]) & (c[None, :] < col_start[:, None] + NA_KW)
    dx = jnp.clip(c[None, :] - c[:, None], -(NA_KW - 1), NA_KW - 1) + (NA_KW - 1)
    onehot = (dx[None] == jnp.arange(2 * NA_KW - 1)[:, None, None]).astype(F32)
    by_dy = jnp.einsum("hyx,xqw->hyqw", rpb, onehot, precision=lax.Precision.HIGHEST)
    by_dy = jnp.where(col_mask[None, None], by_dy, NEG_BIG)
    b = jnp.stack([by_dy[:, delta:delta + kh] for delta in range(NA_KH)], axis=0)
    b = jnp.transpose(b, (0, 1, 3, 2, 4))
    return b.reshape(NA_KH, HEADS * GRID_W, kh * GRID_W).astype(F32)


def _seg_sum(x, seg):
    hi, lo = _split(x)
    return _mm(hi, seg) + _mm(lo, seg)


def _seg_mean(x, seg):
    return _seg_sum(x, seg) * (1.0 / HEAD_DIM)


def _merge_body(x_ref, conv_ref, cprev_ref, cnext_ref, rkv_ref, lora_ref, yf_ref, yb_ref, na_ref, sgu_ref,
                seg_ref, convw_ref, gup_ref, rk_ref, lnw_ref, lnb_ref, sgn_ref, sgw_ref, sgb_ref,
                mg_ref, wout_ref, gpost_ref, o_ref, *, tiles_per_seq):
    tm = ROW_TILE
    i = pl.program_id(0)
    seg = seg_ref[...]

    conv = conv_ref[...].astype(F32)
    z = conv[:, 2 * GROUP_W:] * conv[:, :GROUP_W]
    zp = cprev_ref[15:16, :].astype(F32)
    zp = zp[:, 2 * GROUP_W:] * zp[:, :GROUP_W]
    zn = cnext_ref[0:1, :].astype(F32)
    zn = zn[:, 2 * GROUP_W:] * zn[:, :GROUP_W]
    first = (i % tiles_per_seq) == 0
    last = (i % tiles_per_seq) == tiles_per_seq - 1
    zp = jnp.where(first, 0.0, zp)
    zn = jnp.where(last, 0.0, zn)
    ridx = lax.broadcasted_iota(jnp.int32, (tm, 1), 0)
    z_prev = jnp.where(ridx == 0, zp, pltpu.roll(z, 1, axis=0))
    z_next = jnp.where(ridx == tm - 1, zn, pltpu.roll(z, tm - 1, axis=0))
    cw = convw_ref[...]
    y_conv = conv[:, GROUP_W:2 * GROUP_W] * (z_prev * cw[0:1] + z * cw[1:2] + z_next * cw[2:3])

    lane = lax.broadcasted_iota(jnp.int32, (1, GROUP_W), 1) // HEAD_DIM
    sgw = sgw_ref[...]

    def mixers(rows):
        rkv = rkv_ref[rows, :].astype(F32)
        r, k, v = rkv[:, :GROUP_W], rkv[:, GROUP_W:2 * GROUP_W], rkv[:, 2 * GROUP_W:]
        y = yf_ref[rows, :] + yb_ref[rows, :]
        mu = _seg_mean(y, seg)
        yc = y - mu
        var = _seg_mean(yc * yc, seg)
        yn = yc * lax.rsqrt(var + GN_EPS) * lnw_ref[...] + lnb_ref[...]
        bonus = _seg_sum(r * k * rk_ref[...], seg) * v
        gate = _dot(_sigmoid(lora_ref[rows, :]), gup_ref[...])
        y_rwkv = (yn + bonus) * gate

        sg = sgu_ref[rows, :].astype(F32)
        u = jax.nn.gelu(sg[:, :GROUP_W])
        gv = jax.nn.gelu(sg[:, GROUP_W:])
        gmu = jnp.mean(gv, axis=-1, keepdims=True)
        gc = gv - gmu
        gvn = gc * lax.rsqrt(jnp.mean(gc * gc, axis=-1, keepdims=True) + NORM_EPS) * sgn_ref[...]
        mixed = []
        for ci in range((rows.stop - rows.start) // SGU_CHUNK):
            res = _dot(sgw, gvn[ci * SGU_CHUNK:(ci + 1) * SGU_CHUNK])
            m = res[0:SGU_CHUNK]
            for h in range(1, HEADS):
                m = jnp.where(lane == h, res[h * SGU_CHUNK:(h + 1) * SGU_CHUNK], m)
            mixed.append(m + sgb_ref[...])
        y_sgu = u * jnp.concatenate(mixed, axis=0)

        y_na = na_ref[rows, :].astype(F32)
        merged = jnp.concatenate([_rms(y_conv[rows]), _rms(y_rwkv), _rms(y_na), _rms(y_sgu)], axis=1)
        return (merged * mg_ref[...]).astype(BF16)

    half = tm // 2
    for r0 in (0, half):
        rows = slice(r0, r0 + half)
        out = jnp.dot(mixers(rows), wout_ref[...], preferred_element_type=F32)
        o_ref[rows, :] = x_ref[rows, :] + _rms(out) * gpost_ref[...]


def _merge(x2, conv, rkv, lora, yf, yb, y_na, sgu, batch, seg, convw, gup, rk, lnw, lnb, sgn, sgw, sgb,
           mg, wout, gpost):
    n = x2.shape[0]
    tm = ROW_TILE
    tiles_per_seq = n // batch // tm
    nh = n // 16
    row = lambda wdt: pl.BlockSpec((tm, wdt), lambda i: (i, 0))
    prev = pl.BlockSpec((16, 3 * GROUP_W), lambda i: (jnp.maximum(i * (tm // 16) - 1, 0), 0))
    nxt = pl.BlockSpec((16, 3 * GROUP_W), lambda i: (jnp.minimum((i + 1) * (tm // 16), nh - 1), 0))
    params = (seg, convw, gup, rk, lnw, lnb, sgn, sgw, sgb, mg, wout, gpost)
    return pl.pallas_call(
        functools.partial(_merge_body, tiles_per_seq=tiles_per_seq),
        grid=(n // tm,),
        in_specs=[row(D_MODEL), row(3 * GROUP_W), prev, nxt, row(3 * GROUP_W), row(LORA_W),
                  row(GROUP_W), row(GROUP_W), row(GROUP_W), row(2 * GROUP_W)]
                 + [_full(p.shape) for p in params],
        out_specs=row(D_MODEL),
        out_shape=jax.ShapeDtypeStruct((n, D_MODEL), F32),
        compiler_params=pltpu.CompilerParams(dimension_semantics=("parallel",),
                                             vmem_limit_bytes=VMEM_LIMIT),
        name="merge",
    )(x2, conv, conv, conv, rkv, lora, yf, yb, y_na, sgu, *params)


FFN_SLABS = ((0, 768), (768, 768), (1536, 640), (2176, 640))
assert sum(w for _, w in FFN_SLABS) == D_FF
HALO = 8


def _ffn_body(x_ref, xprev_ref, xnext_ref, gpre_ref, wup_ref, cw_ref, wdn_ref, gpost_ref, o_ref, *,
              tiles_per_seq):
    tm = ROW_TILE
    i = pl.program_id(0)
    first = (i % tiles_per_seq) == 0
    last = (i % tiles_per_seq) == tiles_per_seq - 1
    x = x_ref[...]
    g = gpre_ref[...]
    hf = _rms(x) * g
    hp = jnp.where(first, 0.0, _rms(xprev_ref[...]) * g)
    hn = jnp.where(last, 0.0, _rms(xnext_ref[...]) * g)
    h = hf.astype(BF16)
    hext = jnp.concatenate([hp, hf, hn], axis=0).astype(BF16)
    ext = tm + 2 * HALO

    def up(s):
        c0, w = FFN_SLABS[s]
        gate = jnp.dot(hext, wup_ref[:, c0:c0 + w], preferred_element_type=F32)
        lin = jnp.dot(h, wup_ref[:, D_FF + c0:D_FF + c0 + w], preferred_element_type=F32)
        return gate, lin

    def down(s, gate, lin):
        c0, w = FFN_SLABS[s]
        cw = cw_ref[:, c0:c0 + w]
        g_prev = pltpu.roll(gate, 1, axis=0)[HALO:HALO + tm]
        g_next = pltpu.roll(gate, ext - 1, axis=0)[HALO:HALO + tm]
        cv = g_prev * cw[0:1] + gate[HALO:HALO + tm] * cw[1:2] + g_next * cw[2:3]
        hid = jax.nn.gelu(cv) * lin
        return jnp.dot(hid.astype(BF16), wdn_ref[c0:c0 + w, :], preferred_element_type=F32)

    acc = None
    pending = up(0)
    for s in range(len(FFN_SLABS)):
        nxt = up(s + 1) if s + 1 < len(FFN_SLABS) else None
        part = down(s, *pending)
        acc = part if acc is None else acc + part
        pending = nxt
    o_ref[...] = x + _rms(acc) * gpost_ref[...]


def _ffn(x2, batch, gpre, wup, cw, wdn, gpost):
    n = x2.shape[0]
    tm = ROW_TILE
    tiles_per_seq = n // batch // tm
    nh = n // HALO
    row = pl.BlockSpec((tm, D_MODEL), lambda i: (i, 0))
    prev = pl.BlockSpec((HALO, D_MODEL), lambda i: (jnp.maximum(i * (tm // HALO) - 1, 0), 0))
    nxt = pl.BlockSpec((HALO, D_MODEL), lambda i: (jnp.minimum((i + 1) * (tm // HALO), nh - 1), 0))
    params = (gpre, wup, cw, wdn, gpost)
    return pl.pallas_call(
        functools.partial(_ffn_body, tiles_per_seq=tiles_per_seq),
        grid=(n // tm,),
        in_specs=[row, prev, nxt] + [_full(p.shape) for p in params],
        out_specs=row,
        out_shape=jax.ShapeDtypeStruct((n, D_MODEL), F32),
        compiler_params=pltpu.CompilerParams(dimension_semantics=("parallel",),
                                             vmem_limit_bytes=VMEM_LIMIT),
        name="convffn",
    )(x2, x2, x2, *params)


def _regroup_w_in(w):
    g = GROUP_W
    lora0 = 6 * g
    lora1 = lora0 + 2 * DECAY_RANK + 2 * ICLR_RANK + GATE_RANK
    pad = jnp.zeros((w.shape[0], LORA_W - (lora1 - lora0)), w.dtype)
    return jnp.concatenate([w[:, :lora0], w[:, lora1:lora1 + 5 * g], w[:, lora0:lora1], pad], axis=1).astype(BF16)


def _pad_rows(w, start):
    return jnp.zeros((LORA_W, w.shape[-1]), w.dtype).at[start:start + w.shape[0]].set(w)


def _layer_params(l, p, rows):
    wup = jnp.stack([_pad_rows(p["rwkv_w_up"][l, d], d * DECAY_RANK) for d in range(2)])
    wup_hi = wup.astype(BF16)
    wup = jnp.stack([wup_hi, (wup - wup_hi.astype(F32)).astype(BF16)])
    aup = jnp.stack([_pad_rows(p["rwkv_a_up"][l, d], 2 * DECAY_RANK + d * ICLR_RANK)
                     for d in range(2)]).astype(BF16)
    gup = _pad_rows(p["rwkv_g_up"][l], 2 * DECAY_RANK + 2 * ICLR_RANK).astype(BF16)
    row2 = lambda a: a.reshape(1, -1)
    sgb = jnp.repeat(p["sgu_b"][l].T, HEAD_DIM, axis=1)
    return dict(
        w_in=_regroup_w_in(p["w_in"][l]), g_mix_pre=row2(p["norm_mix_pre"][l]),
        w0=p["rwkv_w0"][l].reshape(2, 1, GROUP_W), wup=wup, a0=p["rwkv_a0"][l].reshape(2, 1, GROUP_W), aup=aup,
        k_k=row2(p["rwkv_k_k"][l]), k_a=row2(p["rwkv_k_a"][l]),
        bias=_na_bias_table(p["na_rpb"][l], rows),
        convw=p["conv_a_w"][l], gup=gup, rk=row2(p["rwkv_r_k"][l]), lnw=row2(p["rwkv_lnx_w"][l]),
        lnb=row2(p["rwkv_lnx_b"][l]), sgn=row2(p["sgu_norm"][l]),
        sgw=p["sgu_w"][l].reshape(HEADS * SGU_CHUNK, SGU_CHUNK).astype(BF16), sgb=sgb,
        mg=row2(p["merge_gain"][l]), wout=p["w_out"][l].astype(BF16), g_mix_post=row2(p["norm_mix_post"][l]),
        g_ffn_pre=row2(p["norm_ffn_pre"][l]), ffn_up=p["ffn_w_up"][l].astype(BF16), ffn_conv=p["ffn_conv"][l],
        ffn_down=p["ffn_w_down"][l].astype(BF16), g_ffn_post=row2(p["norm_ffn_post"][l]),
    )


def _constants():
    i = jnp.arange(WKV_BLOCK)
    same = (i[None, :] // WKV_CHUNK) == (i[:, None] // WKV_CHUNK)
    tri = jnp.stack([same & (i[None, :] <= i[:, None]),
                     same & (i[None, :] >= i[:, None])]).astype(BF16)
    g = jnp.arange(GROUP_W) // HEAD_DIM
    seg = (g[:, None] == g[None, :]).astype(BF16)
    return tri, seg


def _forward(x, p):
    batch, t, _ = x.shape
    assert t % ROW_TILE == 0 and t % (NA_ROWS * GRID_W) == 0 and t // GRID_W >= NA_KH
    x2 = x.reshape(batch * t, D_MODEL)
    tri, seg = _constants()
    for l in range(p["w_in"].shape[0]):
        q = _layer_params(l, p, t // GRID_W)
        conv, rkvk, na, sgu, lora, lw, nalr = _inproj(x2, q["g_mix_pre"], q["w_in"], seg, q["k_k"], q["w0"],
                                                      q["wup"], q["a0"], q["aup"])
        yf, yb = _wkv(rkvk, lw, nalr, batch, tri, q["k_a"])
        y_na = _na(na, q["bias"], batch)
        x2 = _merge(x2, conv, rkvk, lora, yf, yb, y_na, sgu, batch, seg, q["convw"], q["gup"], q["rk"],
                    q["lnw"], q["lnb"], q["sgn"], q["sgw"], q["sgb"], q["mg"], q["wout"], q["g_mix_post"])
        x2 = _ffn(x2, batch, q["g_ffn_pre"], q["ffn_up"], q["ffn_conv"], q["ffn_down"], q["g_ffn_post"])
    return x2.reshape(batch, t, D_MODEL)


def kernel(x, norm_mix_pre, norm_mix_post, norm_ffn_pre, norm_ffn_post, w_in, conv_a_w, rwkv_w0, rwkv_w_up,
           rwkv_a0, rwkv_a_up, rwkv_g_up, rwkv_k_k, rwkv_k_a, rwkv_r_k, rwkv_lnx_w, rwkv_lnx_b, na_rpb,
           sgu_norm, sgu_w, sgu_b, merge_gain, w_out, ffn_w_up, ffn_conv, ffn_w_down):
    p = dict(norm_mix_pre=norm_mix_pre, norm_mix_post=norm_mix_post, norm_ffn_pre=norm_ffn_pre,
             norm_ffn_post=norm_ffn_post, w_in=w_in, conv_a_w=conv_a_w, rwkv_w0=rwkv_w0, rwkv_w_up=rwkv_w_up,
             rwkv_a0=rwkv_a0, rwkv_a_up=rwkv_a_up, rwkv_g_up=rwkv_g_up, rwkv_k_k=rwkv_k_k, rwkv_k_a=rwkv_k_a,
             rwkv_r_k=rwkv_r_k, rwkv_lnx_w=rwkv_lnx_w, rwkv_lnx_b=rwkv_lnx_b, na_rpb=na_rpb, sgu_norm=sgu_norm,
             sgu_w=sgu_w, sgu_b=sgu_b, merge_gain=merge_gain, w_out=w_out, ffn_w_up=ffn_w_up, ffn_conv=ffn_conv,
             ffn_w_down=ffn_w_down)
    return _forward(x, p)
```

```python
import functools
import math

import jax
import jax.numpy as jnp
from jax import lax
from jax.experimental import pallas as pl
from jax.experimental.pallas import tpu as pltpu

F32 = jnp.float32
BF16 = jnp.bfloat16

D_MODEL = 1024
GRID_W = 64
HEADS = 4
HEAD_DIM = 64
GROUP_W = HEADS * HEAD_DIM
DECAY_RANK = 16
ICLR_RANK = 16
GATE_RANK = 32
LORA_W = 128
DECAY_SCALE = math.exp(-0.5)
GN_EPS = 64e-5
NA_KH = 8
NA_KW = 16
SGU_CHUNK = 128
D_FF = 2816
NORM_EPS = 1e-6
NEG_BIG = -1e30

WKV_CHUNK = 64
WKV_BLOCK = 256
ROW_TILE = 512
NA_ROWS = 8
VMEM_LIMIT = 56 * 1024 * 1024


def _dot(a, b):
    return jnp.dot(a.astype(BF16), b.astype(BF16), preferred_element_type=F32)


def _split(x):
    hi = x.astype(BF16)
    lo = (x - hi.astype(F32)).astype(BF16)
    return hi, lo


def _dot_x3(a, b):
    ah, al = _split(a)
    bh, bl = _split(b)
    d = lambda p, q: jnp.dot(p, q, preferred_element_type=F32)
    return d(ah, bh) + (d(ah, bl) + d(al, bh))


def _sigmoid(x):
    return 1.0 / (1.0 + jnp.exp(-x))


def _rms(x, eps=NORM_EPS):
    return x * lax.rsqrt(jnp.mean(x * x, axis=-1, keepdims=True) + eps)


def _full(shape):
    n = len(shape)
    return pl.BlockSpec(shape, lambda *_: (0,) * n)


IN_GROUPS = (3 * GROUP_W, 3 * GROUP_W, 3 * GROUP_W, 2 * GROUP_W, LORA_W)
IN_COLS = sum(IN_GROUPS)
RKVK_W = 4 * GROUP_W


INPROJ_TILE = 1024
INPROJ_PIECE = 256
DECAY_LANES = 2 * DECAY_RANK


def _inproj_body(x_ref, g_ref, w_ref, seg_ref, kk_ref, w0_ref, wd_ref, a0_ref, wa_ref,
                 conv_ref, rkvk_ref, na_ref, sgu_ref, lora_ref, lw_ref, nalr_ref):
    seg = seg_ref[...]
    lane = lax.broadcasted_iota(jnp.int32, (1, LORA_W), 1)

    def project(rows):
        h = _rms(x_ref[rows, :]) * g_ref[...]
        p = jnp.dot(h.astype(BF16), w_ref[...], preferred_element_type=F32)
        o = 0
        for ref, wdt in zip((conv_ref, rkvk_ref, na_ref, sgu_ref, lora_ref), IN_GROUPS):
            ref[rows, 0:wdt] = p[:, o:o + wdt].astype(ref.dtype)
            o += wdt
        return rows, p[:, IN_GROUPS[0] + GROUP_W:IN_GROUPS[0] + 2 * GROUP_W], p[:, IN_COLS - LORA_W:]

    def tokenwise(rows, k, lora):
        kk = k * kk_ref[...]
        rkvk_ref[rows, 3 * GROUP_W:] = (kk * lax.rsqrt(_mm((kk * kk).astype(BF16), seg) + 1e-12)).astype(BF16)
        th = jnp.tanh(lora)
        hi = th.astype(BF16).astype(F32)
        packed = jnp.where(lane < DECAY_LANES, hi,
                           jnp.where(lane < 2 * DECAY_LANES, pltpu.roll(hi, DECAY_LANES, axis=1),
                                     jnp.where(lane < 3 * DECAY_LANES, pltpu.roll(th - hi, 2 * DECAY_LANES, axis=1),
                                               0.0)))
        lw_ref[rows, :] = -DECAY_SCALE * _sigmoid(w0_ref[...] + _mm(packed.astype(BF16), wd_ref[...]))
        nalr_ref[rows, :] = (-_sigmoid(a0_ref[...] + _mm(lora.astype(BF16), wa_ref[...]))).astype(BF16)

    pending = None
    for r0 in range(0, INPROJ_TILE, INPROJ_PIECE):
        done = project(slice(r0, r0 + INPROJ_PIECE))
        if pending is not None:
            tokenwise(*pending)
        pending = done
    tokenwise(*pending)


def _inproj(x2, gain, w, seg, k_k, w0, wup, a0, aup):
    n = x2.shape[0]
    tm = INPROJ_TILE
    row = lambda wdt: pl.BlockSpec((tm, wdt), lambda i: (i, 0))
    params = (gain, w, seg, k_k, w0, wup, a0, aup)
    widths = (IN_GROUPS[0], RKVK_W, IN_GROUPS[2], IN_GROUPS[3], IN_GROUPS[4], 2 * GROUP_W, 2 * GROUP_W)
    dtypes = (BF16, BF16, BF16, BF16, F32, F32, BF16)
    return pl.pallas_call(
        _inproj_body,
        grid=(n // tm,),
        in_specs=[row(D_MODEL)] + [_full(q.shape) for q in params],
        out_specs=[row(w_) for w_ in widths],
        out_shape=[jax.ShapeDtypeStruct((n, w_), dt) for w_, dt in zip(widths, dtypes)],
        compiler_params=pltpu.CompilerParams(dimension_semantics=("parallel",),
                                             vmem_limit_bytes=VMEM_LIMIT),
        name="inproj",
    )(x2, *params)


PAIR_W = 2 * HEAD_DIM
N_PAIRS = HEADS // 2
SOLVE_LEVELS = 6


def _mm(a, b):
    return jnp.dot(a, b, preferred_element_type=F32)


def _mm_nt(a, b):
    return lax.dot_general(a, b, (((1,), (1,)), ((), ())), preferred_element_type=F32)


def _mm_tn(a, b):
    return lax.dot_general(a, b, (((0,), (0,)), ((), ())), preferred_element_type=F32)


def _mm_x3(a, b):
    return _mm(a[0], b[0]) + (_mm(a[0], b[1]) + _mm(a[1], b[0]))


def _wkv_prep(rkvk_ref, lw_ref, nalr_ref, tri, ka_gain, d):
    c = WKV_CHUNK
    pieces = {}
    for ci in range(rkvk_ref.shape[0] // c):
        rows = slice(ci * c, (ci + 1) * c)
        lw = lw_ref[rows, :]
        lw_hi, lw_lo = _split(lw)
        cum = _mm(tri, lw_hi) + _mm(tri, lw_lo)
        g_end = jnp.exp(cum[c - 1:c] if d == 0 else cum[0:1])
        for p in range(N_PAIRS):
            lanes = slice(p * PAIR_W, (p + 1) * PAIR_W)
            part = lambda g: rkvk_ref[rows, g * GROUP_W + p * PAIR_W:g * GROUP_W + (p + 1) * PAIR_W]
            r, k, v, kk = part(0).astype(F32), part(1).astype(F32), part(2), part(3).astype(F32)
            nalr = nalr_ref[rows, lanes].astype(F32)
            k_eff = k * (1.0 - (1.0 + nalr) * ka_gain[:, lanes])
            b_vec = kk * nalr
            g_inc = jnp.exp(cum[:, lanes])
            g_inv = 1.0 / g_inc
            g_hat = g_end[:, lanes] * g_inv
            pieces[ci, p] = dict(
                a_t=kk * jnp.exp(cum[:, lanes] - lw[:, lanes]), r_t=r * g_inc,
                b_t=(b_vec * g_inv).astype(BF16), k_t=(k_eff * g_inv).astype(BF16),
                b_h=(b_vec * g_hat).astype(BF16), k_h=(k_eff * g_hat).astype(BF16),
                v=v, g_end=g_end[:, lanes])
    return pieces


def _wkv_body(rkvk_f, lw_f, nalr_f, rkvk_b, lw_b, nalr_b, tri_ref, ka_ref, yf_ref, yb_ref, h_ref):
    @pl.when(pl.program_id(1) == 0)
    def _():
        h_ref[...] = jnp.zeros_like(h_ref)

    c = WKV_CHUNK
    nch = WKV_BLOCK // c
    in_refs = ((rkvk_f, lw_f, nalr_f), (rkvk_b, lw_b, nalr_b))
    y_refs = (yf_ref, yb_ref)

    def head_lanes(pairs_wide):
        lane = lax.broadcasted_iota(jnp.int32, (1, pairs_wide * PAIR_W), 1)
        return [(lane % PAIR_W) // HEAD_DIM == h for h in range(2)]

    head1, head2, head3 = head_lanes(1), head_lanes(2), head_lanes(3)
    row = lax.broadcasted_iota(jnp.int32, (PAIR_W, PAIR_W), 0)
    col = lax.broadcasted_iota(jnp.int32, (PAIR_W, PAIR_W), 1)
    t_idx, s_idx = row % c, col % c
    tri_mask = [(s_idx < t_idx) | ((row >= c) & (s_idx == t_idx)),
                (s_idx > t_idx) | ((row >= c) & (s_idx == t_idx))]
    diag = row == col
    own_block = (row // c) == (col // c)
    zeros_pair = jnp.zeros((c, PAIR_W), BF16)

    cps = [(ci, p) for ci in range(nch) for p in range(N_PAIRS)]
    sl = lambda ci, p: (slice(ci * c, (ci + 1) * c), slice(p * PAIR_W, (p + 1) * PAIR_W))
    st = [dict(a4={}, xa={}, xv={}, s_bf={}, maps={}) for _ in range(2)]

    def prep_stage(d):
        rkvk_ref, lw_ref, nalr_ref = in_refs[d]
        st[d]["q"] = _wkv_prep(rkvk_ref, lw_ref, nalr_ref, tri_ref[d], ka_ref[...], d)

    def score_stage(d, todo):
        for ci, p in todo:
            q = st[d]["q"][ci, p]
            lhs = jnp.concatenate([q["a_t"], q["r_t"]], axis=0).astype(BF16)
            b_t, k_t = q["b_t"], q["k_t"]
            for h, rhs in enumerate((jnp.concatenate([b_t, k_t], axis=0), jnp.concatenate([k_t, b_t], axis=0))):
                m1 = _mm_nt(jnp.where(head1[h], lhs, 0.0), rhs)
                st[d]["a4"][ci, p, h] = jnp.where(tri_mask[d], m1, 0.0).astype(BF16)

    def rhs_stage(d, todo):
        for ci, p in todo:
            q = st[d]["q"][ci, p]
            v = q["v"]
            top = [st[d]["a4"][ci, p, h][:c] for h in range(2)]
            akv = [_mm(top[0], jnp.concatenate([zeros_pair, v], axis=0)),
                   _mm(top[1], jnp.concatenate([v, zeros_pair], axis=0))]
            st[d]["xa"][ci, p] = q["a_t"]
            st[d]["xv"][ci, p] = jnp.where(head1[0], akv[0], akv[1])
            st[d]["s_bf"][ci, p] = jnp.where(head1[0], top[0], top[1])

    def solve_level(d, j):
        xa, xv, s_bf = st[d]["xa"], st[d]["xv"], st[d]["s_bf"]
        last = j == SOLVE_LEVELS - 1
        heads = head2 if last else head3
        for key in cps:
            parts = ([] if last else [s_bf[key]]) + [xa[key].astype(BF16), xv[key].astype(BF16)]
            cat = jnp.concatenate(parts, axis=1)
            out = _mm(s_bf[key], jnp.concatenate([jnp.where(heads[0], cat, 0.0), jnp.where(heads[1], cat, 0.0)],
                                                 axis=0))
            o = 0
            if not last:
                s_bf[key] = out[:, :PAIR_W].astype(BF16)
                o = PAIR_W
            xa[key] = xa[key] + out[:, o:o + PAIR_W]
            xv[key] = xv[key] + out[:, o + PAIR_W:]

    def map_stage(d, todo):
        for ci, p in todo:
            q = st[d]["q"][ci, p]
            u = jnp.concatenate([st[d]["xa"][ci, p].astype(BF16), st[d]["xv"][ci, p].astype(BF16)], axis=1)
            vz = jnp.concatenate([zeros_pair, q["v"]], axis=1)
            uv = jnp.concatenate([u, vz], axis=0)
            ry = jnp.where(head2[0], _mm(st[d]["a4"][ci, p, 0][c:], uv),
                           _mm(st[d]["a4"][ci, p, 1][c:], jnp.concatenate([vz, u], axis=0)))
            pq = _mm_tn(jnp.concatenate([q["b_h"], q["k_h"]], axis=0), uv)
            rhat = q["r_t"] + ry[:, :PAIR_W]
            pmat = jnp.where(own_block, pq[:, :PAIR_W], 0.0) + jnp.where(diag, q["g_end"], 0.0)
            qmat = jnp.where(own_block, pq[:, PAIR_W:], 0.0)
            st[d]["maps"][ci, p] = (jnp.concatenate([rhat, pmat], axis=0).astype(BF16), ry[:, PAIR_W:], qmat)

    def carry_step(d, step):
        ci = step if d == 0 else nch - 1 - step
        for p in range(N_PAIRS):
            rows, lanes = sl(ci, p)
            lhs, yi, qmat = st[d]["maps"][ci, p]
            out = _mm(lhs, jnp.concatenate(_split(h_ref[d, p]), axis=1))
            out = out[:, :PAIR_W] + out[:, PAIR_W:]
            y_refs[d][rows, lanes] = out[:c] + yi
            h_ref[d, p] = out[c:] + qmat

    for stage in (prep_stage, lambda d: score_stage(d, cps), lambda d: rhs_stage(d, cps)):
        for d in range(2):
            stage(d)
    for j in range(SOLVE_LEVELS):
        for d in range(2):
            solve_level(d, j)
    for d in range(2):
        map_stage(d, cps)
    for step in range(nch):
        for d in range(2):
            carry_step(d, step)


def _wkv(rkvk, lw, nalr, batch, tri, k_a):
    n = rkvk.shape[0]
    nb = n // batch // WKV_BLOCK
    fwd = lambda wdt, col=0: pl.BlockSpec((WKV_BLOCK, wdt), lambda b, j: (b * nb + j, col))
    bwd = lambda wdt, col=0: pl.BlockSpec((WKV_BLOCK, wdt), lambda b, j: (b * nb + nb - 1 - j, col))
    return pl.pallas_call(
        _wkv_body,
        grid=(batch, nb),
        in_specs=[fwd(RKVK_W), fwd(GROUP_W, 0), fwd(GROUP_W, 0), bwd(RKVK_W), bwd(GROUP_W, 1), bwd(GROUP_W, 1),
                  _full(tri.shape), _full(k_a.shape)],
        out_specs=[fwd(GROUP_W), bwd(GROUP_W)],
        out_shape=[jax.ShapeDtypeStruct((n, GROUP_W), F32)] * 2,
        scratch_shapes=[pltpu.VMEM((2, N_PAIRS, PAIR_W, PAIR_W), F32)],
        compiler_params=pltpu.CompilerParams(dimension_semantics=("parallel", "arbitrary"),
                                             vmem_limit_bytes=VMEM_LIMIT),
        name="wkv7",
    )(rkvk, lw, nalr, rkvk, lw, nalr, tri, k_a)


def _na_body(q_ref, k_ref, v_ref, bias_ref, o_ref, *, rows):
    j = pl.program_id(1)
    band = NA_KH * GRID_W
    lane = lax.broadcasted_iota(jnp.int32, (1, GROUP_W), 1) // HEAD_DIM
    starts, scores = [], []
    for rr in range(NA_ROWS):
        r = j * NA_ROWS + rr
        rs = jnp.clip(r - NA_KH // 2, 0, rows - NA_KH)
        start = pl.multiple_of(rs * GRID_W, GRID_W)
        q = q_ref[rr * GRID_W:(rr + 1) * GRID_W, :] * (HEAD_DIM ** -0.5)
        qs = jnp.concatenate([jnp.where(lane == h, q, 0.0) for h in range(HEADS)], axis=0).astype(BF16)
        s = _mm_nt(qs, k_ref[pl.ds(start, band), :])
        starts.append(start)
        scores.append(s + bias_ref[rs - r + (NA_KH - 1)])
    probs, norms = [], []
    for s in scores:
        p = jnp.exp(s - jnp.max(s, axis=-1, keepdims=True))
        norms.append(jnp.sum(p, axis=-1, keepdims=True))
        probs.append(p.astype(BF16))
    for rr in range(NA_ROWS):
        o = _mm(probs[rr], v_ref[pl.ds(starts[rr], band), :]) / norms[rr]
        acc = o[0:GRID_W]
        for h in range(1, HEADS):
            acc = jnp.where(lane == h, o[h * GRID_W:(h + 1) * GRID_W], acc)
        o_ref[rr * GRID_W:(rr + 1) * GRID_W, :] = acc.astype(o_ref.dtype)


def _na(na, bias, batch):
    n = na.shape[0]
    t = n // batch
    rows = t // GRID_W
    nj = rows // NA_ROWS
    blk = NA_ROWS * GRID_W
    return pl.pallas_call(
        functools.partial(_na_body, rows=rows),
        grid=(batch, nj),
        in_specs=[pl.BlockSpec((blk, GROUP_W), lambda b, j: (b * nj + j, 0)),
                  pl.BlockSpec((t, GROUP_W), lambda b, j: (b, 1)),
                  pl.BlockSpec((t, GROUP_W), lambda b, j: (b, 2)),
                  _full(bias.shape)],
        out_specs=pl.BlockSpec((blk, GROUP_W), lambda b, j: (b * nj + j, 0)),
        out_shape=jax.ShapeDtypeStruct((n, GROUP_W), BF16),
        compiler_params=pltpu.CompilerParams(dimension_semantics=("parallel", "arbitrary"),
                                             vmem_limit_bytes=VMEM_LIMIT),
        name="nattn",
    )(na, na, na, bias)


def _na_bias_table(rpb, rows):
    kh = min(NA_KH, rows)
    c = jnp.arange(GRID_W)
    col_start = jnp.clip(c - NA_KW // 2, 0, GRID_W - NA_KW)
    col_mask = (c[None, :] >= col_start[:, None]) & (c[None, :] < col_start[:, None] + NA_KW)
    dx = jnp.clip(c[None, :] - c[:, None], -(NA_KW - 1), NA_KW - 1) + (NA_KW - 1)
    onehot = (dx[None] == jnp.arange(2 * NA_KW - 1)[:, None, None]).astype(F32)
    by_dy = jnp.einsum("hyx,xqw->hyqw", rpb, onehot, precision=lax.Precision.HIGHEST)
    by_dy = jnp.where(col_mask[None, None], by_dy, NEG_BIG)
    b = jnp.stack([by_dy[:, delta:delta + kh] for delta in range(NA_KH)], axis=0)
    b = jnp.transpose(b, (0, 1, 3, 2, 4))
    return b.reshape(NA_KH, HEADS * GRID_W, kh * GRID_W).astype(F32)


def _seg_sum(x, seg):
    hi, lo = _split(x)
    return _mm(hi, seg) + _mm(lo, seg)


def _seg_mean(x, seg):
    return _seg_sum(x, seg) * (1.0 / HEAD_DIM)


def _merge_body(x_ref, conv_ref, cprev_ref, cnext_ref, rkv_ref, lora_ref, yf_ref, yb_ref, na_ref, sgu_ref,
                seg_ref, convw_ref, gup_ref, rk_ref, lnw_ref, lnb_ref, sgn_ref, sgw_ref, sgb_ref,
                mg_ref, wout_ref, gpost_ref, o_ref, *, tiles_per_seq):
    tm = ROW_TILE
    i = pl.program_id(0)
    seg = seg_ref[...]

    conv = conv_ref[...].astype(F32)
    z = conv[:, 2 * GROUP_W:] * conv[:, :GROUP_W]
    zp = cprev_ref[15:16, :].astype(F32)
    zp = zp[:, 2 * GROUP_W:] * zp[:, :GROUP_W]
    zn = cnext_ref[0:1, :].astype(F32)
    zn = zn[:, 2 * GROUP_W:] * zn[:, :GROUP_W]
    first = (i % tiles_per_seq) == 0
    last = (i % tiles_per_seq) == tiles_per_seq - 1
    zp = jnp.where(first, 0.0, zp)
    zn = jnp.where(last, 0.0, zn)
    ridx = lax.broadcasted_iota(jnp.int32, (tm, 1), 0)
    z_prev = jnp.where(ridx == 0, zp, pltpu.roll(z, 1, axis=0))
    z_next = jnp.where(ridx == tm - 1, zn, pltpu.roll(z, tm - 1, axis=0))
    cw = convw_ref[...]
    y_conv = conv[:, GROUP_W:2 * GROUP_W] * (z_prev * cw[0:1] + z * cw[1:2] + z_next * cw[2:3])

    lane = lax.broadcasted_iota(jnp.int32, (1, GROUP_W), 1) // HEAD_DIM
    sgw = sgw_ref[...]

    def mixers(rows):
        rkv = rkv_ref[rows, :].astype(F32)
        r, k, v = rkv[:, :GROUP_W], rkv[:, GROUP_W:2 * GROUP_W], rkv[:, 2 * GROUP_W:]
        y = yf_ref[rows, :] + yb_ref[rows, :]
        mu = _seg_mean(y, seg)
        yc = y - mu
        var = _seg_mean(yc * yc, seg)
        yn = yc * lax.rsqrt(var + GN_EPS) * lnw_ref[...] + lnb_ref[...]
        bonus = _seg_sum(r * k * rk_ref[...], seg) * v
        gate = _dot(_sigmoid(lora_ref[rows, :]), gup_ref[...])
        y_rwkv = (yn + bonus) * gate

        sg = sgu_ref[rows, :].astype(F32)
        u = jax.nn.gelu(sg[:, :GROUP_W])
        gv = jax.nn.gelu(sg[:, GROUP_W:])
        gmu = jnp.mean(gv, axis=-1, keepdims=True)
        gc = gv - gmu
        gvn = gc * lax.rsqrt(jnp.mean(gc * gc, axis=-1, keepdims=True) + NORM_EPS) * sgn_ref[...]
        mixed = []
        for ci in range((rows.stop - rows.start) // SGU_CHUNK):
            res = _dot(sgw, gvn[ci * SGU_CHUNK:(ci + 1) * SGU_CHUNK])
            m = res[0:SGU_CHUNK]
            for h in range(1, HEADS):
                m = jnp.where(lane == h, res[h * SGU_CHUNK:(h + 1) * SGU_CHUNK], m)
            mixed.append(m + sgb_ref[...])
        y_sgu = u * jnp.concatenate(mixed, axis=0)

        y_na = na_ref[rows, :].astype(F32)
        merged = jnp.concatenate([_rms(y_conv[rows]), _rms(y_rwkv), _rms(y_na), _rms(y_sgu)], axis=1)
        return (merged * mg_ref[...]).astype(BF16)

    half = tm // 2
    for r0 in (0, half):
        rows = slice(r0, r0 + half)
        out = jnp.dot(mixers(rows), wout_ref[...], preferred_element_type=F32)
        o_ref[rows, :] = x_ref[rows, :] + _rms(out) * gpost_ref[...]


def _merge(x2, conv, rkv, lora, yf, yb, y_na, sgu, batch, seg, convw, gup, rk, lnw, lnb, sgn, sgw, sgb,
           mg, wout, gpost):
    n = x2.shape[0]
    tm = ROW_TILE
    tiles_per_seq = n // batch // tm
    nh = n // 16
    row = lambda wdt: pl.BlockSpec((tm, wdt), lambda i: (i, 0))
    prev = pl.BlockSpec((16, 3 * GROUP_W), lambda i: (jnp.maximum(i * (tm // 16) - 1, 0), 0))
    nxt = pl.BlockSpec((16, 3 * GROUP_W), lambda i: (jnp.minimum((i + 1) * (tm // 16), nh - 1), 0))
    params = (seg, convw, gup, rk, lnw, lnb, sgn, sgw, sgb, mg, wout, gpost)
    return pl.pallas_call(
        functools.partial(_merge_body, tiles_per_seq=tiles_per_seq),
        grid=(n // tm,),
        in_specs=[row(D_MODEL), row(3 * GROUP_W), prev, nxt, row(3 * GROUP_W), row(LORA_W),
                  row(GROUP_W), row(GROUP_W), row(GROUP_W), row(2 * GROUP_W)]
                 + [_full(p.shape) for p in params],
        out_specs=row(D_MODEL),
        out_shape=jax.ShapeDtypeStruct((n, D_MODEL), F32),
        compiler_params=pltpu.CompilerParams(dimension_semantics=("parallel",),
                                             vmem_limit_bytes=VMEM_LIMIT),
        name="merge",
    )(x2, conv, conv, conv, rkv, lora, yf, yb, y_na, sgu, *params)


FFN_SLABS = ((0, 768), (768, 768), (1536, 768), (2304, 512))
assert sum(w for _, w in FFN_SLABS) == D_FF
HALO = 8


def _ffn_body(x_ref, xprev_ref, xnext_ref, gpre_ref, wup_ref, cw_ref, wdn_ref, gpost_ref, o_ref, *,
              tiles_per_seq):
    tm = ROW_TILE
    i = pl.program_id(0)
    first = (i % tiles_per_seq) == 0
    last = (i % tiles_per_seq) == tiles_per_seq - 1
    x = x_ref[...]
    g = gpre_ref[...]
    hf = _rms(x) * g
    hp = jnp.where(first, 0.0, _rms(xprev_ref[...]) * g)
    hn = jnp.where(last, 0.0, _rms(xnext_ref[...]) * g)
    h = hf.astype(BF16)
    hext = jnp.concatenate([hp, hf, hn], axis=0).astype(BF16)
    ext = tm + 2 * HALO

    def up(s):
        c0, w = FFN_SLABS[s]
        gate = jnp.dot(hext, wup_ref[:, c0:c0 + w], preferred_element_type=F32)
        lin = jnp.dot(h, wup_ref[:, D_FF + c0:D_FF + c0 + w], preferred_element_type=F32)
        return gate, lin

    def down(s, gate, lin):
        c0, w = FFN_SLABS[s]
        cw = cw_ref[:, c0:c0 + w]
        g_prev = pltpu.roll(gate, 1, axis=0)[HALO:HALO + tm]
        g_next = pltpu.roll(gate, ext - 1, axis=0)[HALO:HALO + tm]
        cv = g_prev * cw[0:1] + gate[HALO:HALO + tm] * cw[1:2] + g_next * cw[2:3]
        hid = jax.nn.gelu(cv) * lin
        return jnp.dot(hid.astype(BF16), wdn_ref[c0:c0 + w, :], preferred_element_type=F32)

    acc = None
    pending = up(0)
    for s in range(len(FFN_SLABS)):
        nxt = up(s + 1) if s + 1 < len(FFN_SLABS) else None
        part = down(s, *pending)
        acc = part if acc is None else acc + part
        pending = nxt
    o_ref[...] = x + _rms(acc) * gpost_ref[...]


def _ffn(x2, batch, gpre, wup, cw, wdn, gpost):
    n = x2.shape[0]
    tm = ROW_TILE
    tiles_per_seq = n // batch // tm
    nh = n // HALO
    row = pl.BlockSpec((tm, D_MODEL), lambda i: (i, 0))
    prev = pl.BlockSpec((HALO, D_MODEL), lambda i: (jnp.maximum(i * (tm // HALO) - 1, 0), 0))
    nxt = pl.BlockSpec((HALO, D_MODEL), lambda i: (jnp.minimum((i + 1) * (tm // HALO), nh - 1), 0))
    params = (gpre, wup, cw, wdn, gpost)
    return pl.pallas_call(
        functools.partial(_ffn_body, tiles_per_seq=tiles_per_seq),
        grid=(n // tm,),
        in_specs=[row, prev, nxt] + [_full(p.shape) for p in params],
        out_specs=row,
        out_shape=jax.ShapeDtypeStruct((n, D_MODEL), F32),
        compiler_params=pltpu.CompilerParams(dimension_semantics=("parallel",),
                                             vmem_limit_bytes=VMEM_LIMIT),
        name="convffn",
    )(x2, x2, x2, *params)


def _regroup_w_in(w):
    g = GROUP_W
    lora0 = 6 * g
    lora1 = lora0 + 2 * DECAY_RANK + 2 * ICLR_RANK + GATE_RANK
    pad = jnp.zeros((w.shape[0], LORA_W - (lora1 - lora0)), w.dtype)
    return jnp.concatenate([w[:, :lora0], w[:, lora1:lora1 + 5 * g], w[:, lora0:lora1], pad], axis=1).astype(BF16)


def _pad_rows(w, start):
    return jnp.zeros((LORA_W, w.shape[-1]), w.dtype).at[start:start + w.shape[0]].set(w)


def _layer_params(l, p, rows):
    both = lambda w, start: jnp.concatenate(
        [_pad_rows(w[d], start + d * w.shape[1]) for d in range(2)], axis=1)
    wd = both(p["rwkv_w_up"][l], 0)[:DECAY_LANES]
    wd_hi = wd.astype(BF16)
    wd_lo = (wd - wd_hi.astype(F32)).astype(BF16)
    wup = jnp.concatenate([wd_hi, wd_lo, wd_hi, jnp.zeros_like(wd_hi)], axis=0)
    aup = both(p["rwkv_a_up"][l], 2 * DECAY_RANK).astype(BF16)
    gup = _pad_rows(p["rwkv_g_up"][l], 2 * DECAY_RANK + 2 * ICLR_RANK).astype(BF16)
    row2 = lambda a: a.reshape(1, -1)
    sgb = jnp.repeat(p["sgu_b"][l].T, HEAD_DIM, axis=1)
    return dict(
        w_in=_regroup_w_in(p["w_in"][l]), g_mix_pre=row2(p["norm_mix_pre"][l]),
        w0=row2(p["rwkv_w0"][l]), wup=wup, a0=row2(p["rwkv_a0"][l]), aup=aup,
        k_k=row2(p["rwkv_k_k"][l]), k_a=row2(p["rwkv_k_a"][l]),
        bias=_na_bias_table(p["na_rpb"][l], rows),
        convw=p["conv_a_w"][l], gup=gup, rk=row2(p["rwkv_r_k"][l]), lnw=row2(p["rwkv_lnx_w"][l]),
        lnb=row2(p["rwkv_lnx_b"][l]), sgn=row2(p["sgu_norm"][l]),
        sgw=p["sgu_w"][l].reshape(HEADS * SGU_CHUNK, SGU_CHUNK).astype(BF16), sgb=sgb,
        mg=row2(p["merge_gain"][l]), wout=p["w_out"][l].astype(BF16), g_mix_post=row2(p["norm_mix_post"][l]),
        g_ffn_pre=row2(p["norm_ffn_pre"][l]), ffn_up=p["ffn_w_up"][l].astype(BF16), ffn_conv=p["ffn_conv"][l],
        ffn_down=p["ffn_w_down"][l].astype(BF16), g_ffn_post=row2(p["norm_ffn_post"][l]),
    )


def _constants():
    i = jnp.arange(WKV_CHUNK)
    tri = jnp.stack([i[None, :] <= i[:, None], i[None, :] >= i[:, None]]).astype(BF16)
    g = jnp.arange(GROUP_W) // HEAD_DIM
    seg = (g[:, None] == g[None, :]).astype(BF16)
    return tri, seg


def _forward(x, p):
    batch, t, _ = x.shape
    assert t % ROW_TILE == 0 and t % (NA_ROWS * GRID_W) == 0 and t // GRID_W >= NA_KH
    assert (batch * t) % INPROJ_TILE == 0 and t % WKV_BLOCK == 0
    x2 = x.reshape(batch * t, D_MODEL)
    tri, seg = _constants()
    for l in range(p["w_in"].shape[0]):
        q = _layer_params(l, p, t // GRID_W)
        conv, rkvk, na, sgu, lora, lw, nalr = _inproj(x2, q["g_mix_pre"], q["w_in"], seg, q["k_k"], q["w0"],
                                                      q["wup"], q["a0"], q["aup"])
        yf, yb = _wkv(rkvk, lw, nalr, batch, tri, q["k_a"])
        y_na = _na(na, q["bias"], batch)
        x2 = _merge(x2, conv, rkvk, lora, yf, yb, y_na, sgu, batch, seg, q["convw"], q["gup"], q["rk"],
                    q["lnw"], q["lnb"], q["sgn"], q["sgw"], q["sgb"], q["mg"], q["wout"], q["g_mix_post"])
        x2 = _ffn(x2, batch, q["g_ffn_pre"], q["ffn_up"], q["ffn_conv"], q["ffn_down"], q["g_ffn_post"])
    return x2.reshape(batch, t, D_MODEL)


def kernel(x, norm_mix_pre, norm_mix_post, norm_ffn_pre, norm_ffn_post, w_in, conv_a_w, rwkv_w0, rwkv_w_up,
           rwkv_a0, rwkv_a_up, rwkv_g_up, rwkv_k_k, rwkv_k_a, rwkv_r_k, rwkv_lnx_w, rwkv_lnx_b, na_rpb,
           sgu_norm, sgu_w, sgu_b, merge_gain, w_out, ffn_w_up, ffn_conv, ffn_w_down):
    p = dict(norm_mix_pre=norm_mix_pre, norm_mix_post=norm_mix_post, norm_ffn_pre=norm_ffn_pre,
             norm_ffn_post=norm_ffn_post, w_in=w_in, conv_a_w=conv_a_w, rwkv_w0=rwkv_w0, rwkv_w_up=rwkv_w_up,
             rwkv_a0=rwkv_a0, rwkv_a_up=rwkv_a_up, rwkv_g_up=rwkv_g_up, rwkv_k_k=rwkv_k_k, rwkv_k_a=rwkv_k_a,
             rwkv_r_k=rwkv_r_k, rwkv_lnx_w=rwkv_lnx_w, rwkv_lnx_b=rwkv_lnx_b, na_rpb=na_rpb, sgu_norm=sgu_norm,
             sgu_w=sgu_w, sgu_b=sgu_b, merge_gain=merge_gain, w_out=w_out, ffn_w_up=ffn_w_up, ffn_conv=ffn_conv,
             ffn_w_down=ffn_w_down)
    return _forward(x, p)
```

```python
import functools
import math

import jax
import jax.numpy as jnp
from jax import lax
from jax.experimental import pallas as pl
from jax.experimental.pallas import tpu as pltpu

F32 = jnp.float32
BF16 = jnp.bfloat16

D_MODEL = 1024
GRID_W = 64
HEADS = 4
HEAD_DIM = 64
GROUP_W = HEADS * HEAD_DIM
DECAY_RANK = 16
ICLR_RANK = 16
GATE_RANK = 32
LORA_W = 128
DECAY_SCALE = math.exp(-0.5)
GN_EPS = 64e-5
NA_KH = 8
NA_KW = 16
SGU_CHUNK = 128
D_FF = 2816
NORM_EPS = 1e-6
NEG_BIG = -1e30

WKV_CHUNK = 64
WKV_BLOCK = 256
ROW_TILE = 512
NA_ROWS = 8
VMEM_LIMIT = 56 * 1024 * 1024


def _dot(a, b):
    return jnp.dot(a.astype(BF16), b.astype(BF16), preferred_element_type=F32)


def _split(x):
    hi = x.astype(BF16)
    lo = (x - hi.astype(F32)).astype(BF16)
    return hi, lo


def _dot_x3(a, b):
    ah, al = _split(a)
    bh, bl = _split(b)
    d = lambda p, q: jnp.dot(p, q, preferred_element_type=F32)
    return d(ah, bh) + (d(ah, bl) + d(al, bh))


def _sigmoid(x):
    return 1.0 / (1.0 + jnp.exp(-x))


def _rms(x, eps=NORM_EPS):
    return x * lax.rsqrt(jnp.mean(x * x, axis=-1, keepdims=True) + eps)


def _full(shape):
    n = len(shape)
    return pl.BlockSpec(shape, lambda *_: (0,) * n)


IN_GROUPS = (3 * GROUP_W, 3 * GROUP_W, 3 * GROUP_W, 2 * GROUP_W, LORA_W)
IN_COLS = sum(IN_GROUPS)
RKVK_W = 4 * GROUP_W


INPROJ_TILE = 1024
INPROJ_PIECE = 256
DECAY_LANES = 2 * DECAY_RANK


def _inproj_body(x_ref, g_ref, w_ref, seg_ref, kk_ref, w0_ref, wd_ref, a0_ref, wa_ref,
                 conv_ref, rkvk_ref, na_ref, sgu_ref, lora_ref, lw_ref, nalr_ref):
    seg = seg_ref[...]
    lane = lax.broadcasted_iota(jnp.int32, (1, LORA_W), 1)

    def project(rows):
        h = _rms(x_ref[rows, :]) * g_ref[...]
        p = jnp.dot(h.astype(BF16), w_ref[...], preferred_element_type=F32)
        o = 0
        for ref, wdt in zip((conv_ref, rkvk_ref, na_ref, sgu_ref, lora_ref), IN_GROUPS):
            ref[rows, 0:wdt] = p[:, o:o + wdt].astype(ref.dtype)
            o += wdt
        return rows, p[:, IN_GROUPS[0] + GROUP_W:IN_GROUPS[0] + 2 * GROUP_W], p[:, IN_COLS - LORA_W:]

    def tokenwise(rows, k, lora):
        kk = k * kk_ref[...]
        rkvk_ref[rows, 3 * GROUP_W:] = (kk * lax.rsqrt(_mm((kk * kk).astype(BF16), seg) + 1e-12)).astype(BF16)
        th = jnp.tanh(lora)
        hi = th.astype(BF16).astype(F32)
        packed = jnp.where(lane < DECAY_LANES, hi,
                           jnp.where(lane < 2 * DECAY_LANES, pltpu.roll(hi, DECAY_LANES, axis=1),
                                     jnp.where(lane < 3 * DECAY_LANES, pltpu.roll(th - hi, 2 * DECAY_LANES, axis=1),
                                               0.0)))
        lw_ref[rows, :] = -DECAY_SCALE * _sigmoid(w0_ref[...] + _mm(packed.astype(BF16), wd_ref[...]))
        nalr_ref[rows, :] = (-_sigmoid(a0_ref[...] + _mm(lora.astype(BF16), wa_ref[...]))).astype(BF16)

    pending = None
    for r0 in range(0, INPROJ_TILE, INPROJ_PIECE):
        done = project(slice(r0, r0 + INPROJ_PIECE))
        if pending is not None:
            tokenwise(*pending)
        pending = done
    tokenwise(*pending)


def _inproj(x2, gain, w, seg, k_k, w0, wup, a0, aup):
    n = x2.shape[0]
    tm = INPROJ_TILE
    row = lambda wdt: pl.BlockSpec((tm, wdt), lambda i: (i, 0))
    params = (gain, w, seg, k_k, w0, wup, a0, aup)
    widths = (IN_GROUPS[0], RKVK_W, IN_GROUPS[2], IN_GROUPS[3], IN_GROUPS[4], 2 * GROUP_W, 2 * GROUP_W)
    dtypes = (BF16, BF16, BF16, BF16, F32, F32, BF16)
    return pl.pallas_call(
        _inproj_body,
        grid=(n // tm,),
        in_specs=[row(D_MODEL)] + [_full(q.shape) for q in params],
        out_specs=[row(w_) for w_ in widths],
        out_shape=[jax.ShapeDtypeStruct((n, w_), dt) for w_, dt in zip(widths, dtypes)],
        compiler_params=pltpu.CompilerParams(dimension_semantics=("parallel",),
                                             vmem_limit_bytes=VMEM_LIMIT),
        name="inproj",
    )(x2, *params)


PAIR_W = 2 * HEAD_DIM
N_PAIRS = HEADS // 2
SOLVE_LEVELS = 6


def _mm(a, b):
    return jnp.dot(a, b, preferred_element_type=F32)


def _mm_nt(a, b):
    return lax.dot_general(a, b, (((1,), (1,)), ((), ())), preferred_element_type=F32)


def _mm_tn(a, b):
    return lax.dot_general(a, b, (((0,), (0,)), ((), ())), preferred_element_type=F32)


def _mm_x3(a, b):
    return _mm(a[0], b[0]) + (_mm(a[0], b[1]) + _mm(a[1], b[0]))


def _wkv_prep(rkvk_ref, lw_ref, nalr_ref, tri, ka_gain, d, ci):
    c = WKV_CHUNK
    pieces = {}
    rows = slice(ci * c, (ci + 1) * c)
    lw = lw_ref[rows, :]
    lw_hi, lw_lo = _split(lw)
    cum = _mm(tri, lw_hi) + _mm(tri, lw_lo)
    g_end = jnp.exp(cum[c - 1:c] if d == 0 else cum[0:1])
    for p in range(N_PAIRS):
        lanes = slice(p * PAIR_W, (p + 1) * PAIR_W)
        part = lambda g: rkvk_ref[rows, g * GROUP_W + p * PAIR_W:g * GROUP_W + (p + 1) * PAIR_W]
        r, k, v, kk = part(0).astype(F32), part(1).astype(F32), part(2), part(3).astype(F32)
        nalr = nalr_ref[rows, lanes].astype(F32)
        k_eff = k * (1.0 - (1.0 + nalr) * ka_gain[:, lanes])
        b_vec = kk * nalr
        g_inc = jnp.exp(cum[:, lanes])
        g_inv = 1.0 / g_inc
        g_hat = g_end[:, lanes] * g_inv
        pieces[ci, p] = dict(
            a_t=kk * jnp.exp(cum[:, lanes] - lw[:, lanes]), r_t=r * g_inc,
            b_t=(b_vec * g_inv).astype(BF16), k_t=(k_eff * g_inv).astype(BF16),
            b_h=(b_vec * g_hat).astype(BF16), k_h=(k_eff * g_hat).astype(BF16),
            v=v, g_end=g_end[:, lanes])
    return pieces


def _wkv_body(rkvk_f, lw_f, nalr_f, rkvk_b, lw_b, nalr_b, tri_ref, ka_ref, yf_ref, yb_ref, h_ref):
    @pl.when(pl.program_id(1) == 0)
    def _():
        h_ref[...] = jnp.zeros_like(h_ref)

    c = WKV_CHUNK
    nch = WKV_BLOCK // c
    in_refs = ((rkvk_f, lw_f, nalr_f), (rkvk_b, lw_b, nalr_b))
    y_refs = (yf_ref, yb_ref)

    def head_lanes(pairs_wide):
        lane = lax.broadcasted_iota(jnp.int32, (1, pairs_wide * PAIR_W), 1)
        return [(lane % PAIR_W) // HEAD_DIM == h for h in range(2)]

    head1, head2 = head_lanes(1), head_lanes(2)
    row2 = lax.broadcasted_iota(jnp.int32, (PAIR_W, 2 * PAIR_W), 0)
    col2 = lax.broadcasted_iota(jnp.int32, (PAIR_W, 2 * PAIR_W), 1)
    t_idx, s_idx = row2 % c, col2 % c
    tri_mask = [(s_idx < t_idx) | ((row2 >= c) & (s_idx == t_idx)),
                (s_idx > t_idx) | ((row2 >= c) & (s_idx == t_idx))]
    row = lax.broadcasted_iota(jnp.int32, (PAIR_W, PAIR_W), 0)
    col = lax.broadcasted_iota(jnp.int32, (PAIR_W, PAIR_W), 1)
    diag = row == col
    own_block = (row // c) == (col // c)
    zeros_pair = jnp.zeros((c, PAIR_W), BF16)

    cps = [(ci, p) for ci in range(nch) for p in range(N_PAIRS)]
    sl = lambda ci, p: (slice(ci * c, (ci + 1) * c), slice(p * PAIR_W, (p + 1) * PAIR_W))
    st = [dict(q={}, a4={}, x={}, t={}, u={}, s_bf={}, maps={}) for _ in range(2)]
    eye_pair = jnp.where(lax.broadcasted_iota(jnp.int32, (c, PAIR_W), 1) % c
                         == lax.broadcasted_iota(jnp.int32, (c, PAIR_W), 0), 1.0, 0.0)

    def prep_stage(d, ci):
        rkvk_ref, lw_ref, nalr_ref = in_refs[d]
        st[d]["q"].update(_wkv_prep(rkvk_ref, lw_ref, nalr_ref, tri_ref[d], ka_ref[...], d, ci))

    def score_stage(d, todo):
        for ci, p in todo:
            q = st[d]["q"][ci, p]
            lhs = jnp.concatenate([q["a_t"], q["r_t"]], axis=0).astype(BF16)
            b_t, k_t = q["b_t"], q["k_t"]
            rhs = jnp.concatenate([jnp.where(head1[0], jnp.concatenate([b_t, k_t], axis=0), 0.0),
                                   jnp.where(head1[1], jnp.concatenate([k_t, b_t], axis=0), 0.0)], axis=0)
            st[d]["a4"][ci, p] = jnp.where(tri_mask[d], _mm_nt(lhs, rhs), 0.0).astype(BF16)

    def rhs_stage(d, todo):
        for ci, p in todo:
            q = st[d]["q"][ci, p]
            top = st[d]["a4"][ci, p][:c]
            v = [jnp.where(head1[h], q["v"], 0.0) for h in range(2)]
            akv = _mm(top, jnp.concatenate([zeros_pair, v[0], v[1], zeros_pair], axis=0))
            st[d]["x"][ci, p] = jnp.concatenate([q["a_t"].astype(BF16), akv.astype(BF16)], axis=1)
            st[d]["s_bf"][ci, p] = jnp.where(head1[0], top[:, :PAIR_W], top[:, PAIR_W:])
            st[d]["t"][ci, p] = eye_pair

    def by_head(cat, heads):
        return jnp.concatenate([jnp.where(heads[0], cat, 0.0), jnp.where(heads[1], cat, 0.0)], axis=0)

    def solve_level(d, j):
        s_bf, t = st[d]["s_bf"], st[d]["t"]
        last = j == SOLVE_LEVELS - 1
        for key in cps:
            t_bf = t[key].astype(BF16)
            if last:
                t[key] = t[key] + _mm(s_bf[key], by_head(t_bf, head1))
            else:
                out = _mm(s_bf[key], by_head(jnp.concatenate([s_bf[key], t_bf], axis=1), head2))
                s_bf[key] = out[:, :PAIR_W].astype(BF16)
                t[key] = t[key] + out[:, PAIR_W:]

    def apply_stage(d):
        for key in cps:
            st[d]["u"][key] = _mm(st[d]["t"][key].astype(BF16), by_head(st[d]["x"][key], head2)).astype(BF16)

    def map_stage(d, todo):
        for ci, p in todo:
            q = st[d]["q"][ci, p]
            u = st[d]["u"][ci, p]
            vz = jnp.concatenate([zeros_pair, q["v"]], axis=1)
            on = lambda h, t: jnp.where(head2[h], t, 0.0)
            ry = _mm(st[d]["a4"][ci, p][c:], jnp.concatenate([on(0, u), on(0, vz), on(1, vz), on(1, u)], axis=0))
            pq = _mm_tn(jnp.concatenate([q["b_h"], q["k_h"]], axis=0), jnp.concatenate([u, vz], axis=0))
            rhat = q["r_t"] + ry[:, :PAIR_W]
            pmat = jnp.where(own_block, pq[:, :PAIR_W], 0.0) + jnp.where(diag, q["g_end"], 0.0)
            qmat = jnp.where(own_block, pq[:, PAIR_W:], 0.0)
            st[d]["maps"][ci, p] = (jnp.concatenate([rhat, pmat], axis=0).astype(BF16), ry[:, PAIR_W:], qmat)

    def carry_step(d, step):
        ci = step if d == 0 else nch - 1 - step
        for p in range(N_PAIRS):
            rows, lanes = sl(ci, p)
            lhs, yi, qmat = st[d]["maps"][ci, p]
            out = _mm(lhs, jnp.concatenate(_split(h_ref[d, p]), axis=1))
            out = out[:, :PAIR_W] + out[:, PAIR_W:]
            y_refs[d][rows, lanes] = out[:c] + yi
            h_ref[d, p] = out[c:] + qmat

    chunk = lambda ci: [(ci, p) for p in range(N_PAIRS)]
    for ci in range(nch):
        prep_stage(0, ci)
    for ci in range(nch):
        score_stage(0, chunk(ci))
        prep_stage(1, ci)
    for ci in range(nch):
        rhs_stage(0, chunk(ci))
        score_stage(1, chunk(ci))
    rhs_stage(1, cps)
    for j in range(SOLVE_LEVELS):
        for d in range(2):
            solve_level(d, j)
    for d in range(2):
        apply_stage(d)
    for step in range(nch):
        for d in range(2):
            map_stage(d, chunk(step if d == 0 else nch - 1 - step))
        if step:
            for d in range(2):
                carry_step(d, step - 1)
    for d in range(2):
        carry_step(d, nch - 1)


def _wkv(rkvk, lw, nalr, batch, tri, k_a):
    n = rkvk.shape[0]
    nb = n // batch // WKV_BLOCK
    fwd = lambda wdt, col=0: pl.BlockSpec((WKV_BLOCK, wdt), lambda b, j: (b * nb + j, col))
    bwd = lambda wdt, col=0: pl.BlockSpec((WKV_BLOCK, wdt), lambda b, j: (b * nb + nb - 1 - j, col))
    return pl.pallas_call(
        _wkv_body,
        grid=(batch, nb),
        in_specs=[fwd(RKVK_W), fwd(GROUP_W, 0), fwd(GROUP_W, 0), bwd(RKVK_W), bwd(GROUP_W, 1), bwd(GROUP_W, 1),
                  _full(tri.shape), _full(k_a.shape)],
        out_specs=[fwd(GROUP_W), bwd(GROUP_W)],
        out_shape=[jax.ShapeDtypeStruct((n, GROUP_W), F32)] * 2,
        scratch_shapes=[pltpu.VMEM((2, N_PAIRS, PAIR_W, PAIR_W), F32)],
        compiler_params=pltpu.CompilerParams(dimension_semantics=("parallel", "arbitrary"),
                                             vmem_limit_bytes=VMEM_LIMIT),
        name="wkv7",
    )(rkvk, lw, nalr, rkvk, lw, nalr, tri, k_a)


def _na_body(q_ref, k_ref, v_ref, bias_ref, o_ref, *, rows):
    j = pl.program_id(1)
    band = NA_KH * GRID_W
    lane = lax.broadcasted_iota(jnp.int32, (1, GROUP_W), 1) // HEAD_DIM
    starts, scores = [], []
    for rr in range(NA_ROWS):
        r = j * NA_ROWS + rr
        rs = jnp.clip(r - NA_KH // 2, 0, rows - NA_KH)
        start = pl.multiple_of(rs * GRID_W, GRID_W)
        q = q_ref[rr * GRID_W:(rr + 1) * GRID_W, :] * (HEAD_DIM ** -0.5)
        qs = jnp.concatenate([jnp.where(lane == h, q, 0.0) for h in range(HEADS)], axis=0).astype(BF16)
        s = _mm_nt(qs, k_ref[pl.ds(start, band), :])
        starts.append(start)
        scores.append(s + bias_ref[rs - r + (NA_KH - 1)])
    probs, norms = [], []
    for s in scores:
        p = jnp.exp(s - jnp.max(s, axis=-1, keepdims=True))
        norms.append(jnp.sum(p, axis=-1, keepdims=True))
        probs.append(p.astype(BF16))
    for rr in range(NA_ROWS):
        o = _mm(probs[rr], v_ref[pl.ds(starts[rr], band), :]) / norms[rr]
        acc = o[0:GRID_W]
        for h in range(1, HEADS):
            acc = jnp.where(lane == h, o[h * GRID_W:(h + 1) * GRID_W], acc)
        o_ref[rr * GRID_W:(rr + 1) * GRID_W, :] = acc.astype(o_ref.dtype)


def _na(na, bias, batch):
    n = na.shape[0]
    t = n // batch
    rows = t // GRID_W
    nj = rows // NA_ROWS
    blk = NA_ROWS * GRID_W
    return pl.pallas_call(
        functools.partial(_na_body, rows=rows),
        grid=(batch, nj),
        in_specs=[pl.BlockSpec((blk, GROUP_W), lambda b, j: (b * nj + j, 0)),
                  pl.BlockSpec((t, GROUP_W), lambda b, j: (b, 1)),
                  pl.BlockSpec((t, GROUP_W), lambda b, j: (b, 2)),
                  _full(bias.shape)],
        out_specs=pl.BlockSpec((blk, GROUP_W), lambda b, j: (b * nj + j, 0)),
        out_shape=jax.ShapeDtypeStruct((n, GROUP_W), BF16),
        compiler_params=pltpu.CompilerParams(dimension_semantics=("parallel", "arbitrary"),
                                             vmem_limit_bytes=VMEM_LIMIT),
        name="nattn",
    )(na, na, na, bias)


def _na_bias_table(rpb, rows):
    kh = min(NA_KH, rows)
    c = jnp.arange(GRID_W)
    col_start = jnp.clip(c - NA_KW // 2, 0, GRID_W - NA_KW)
    col_mask = (c[None, :] >= col_start[:, None]) & (c[None, :] < col_start[:, None] + NA_KW)
    dx = jnp.clip(c[None, :] - c[:, None], -(NA_KW - 1), NA_KW - 1) + (NA_KW - 1)
    onehot = (dx[None] == jnp.arange(2 * NA_KW - 1)[:, None, None]).astype(F32)
    by_dy = jnp.einsum("hyx,xqw->hyqw", rpb, onehot, precision=lax.Precision.HIGHEST)
    by_dy = jnp.where(col_mask[None, None], by_dy, NEG_BIG)
    b = jnp.stack([by_dy[:, delta:delta + kh] for delta in range(NA_KH)], axis=0)
    b = jnp.transpose(b, (0, 1, 3, 2, 4))
    return b.reshape(NA_KH, HEADS * GRID_W, kh * GRID_W).astype(F32)


def _seg_sum(x, seg):
    hi, lo = _split(x)
    return _mm(hi, seg) + _mm(lo, seg)


def _seg_mean(x, seg):
    return _seg_sum(x, seg) * (1.0 / HEAD_DIM)


def _merge_body(x_ref, conv_ref, cprev_ref, cnext_ref, rkv_ref, lora_ref, yf_ref, yb_ref, na_ref, sgu_ref,
                seg_ref, convw_ref, gup_ref, rk_ref, lnw_ref, lnb_ref, sgn_ref, sgw_ref, sgb_ref,
                mg_ref, wout_ref, gpost_ref, o_ref, *, tiles_per_seq):
    tm = ROW_TILE
    i = pl.program_id(0)
    seg = seg_ref[...]

    conv = conv_ref[...].astype(F32)
    z = conv[:, 2 * GROUP_W:] * conv[:, :GROUP_W]
    zp = cprev_ref[15:16, :].astype(F32)
    zp = zp[:, 2 * GROUP_W:] * zp[:, :GROUP_W]
    zn = cnext_ref[0:1, :].astype(F32)
    zn = zn[:, 2 * GROUP_W:] * zn[:, :GROUP_W]
    first = (i % tiles_per_seq) == 0
    last = (i % tiles_per_seq) == tiles_per_seq - 1
    zp = jnp.where(first, 0.0, zp)
    zn = jnp.where(last, 0.0, zn)
    ridx = lax.broadcasted_iota(jnp.int32, (tm, 1), 0)
    z_prev = jnp.where(ridx == 0, zp, pltpu.roll(z, 1, axis=0))
    z_next = jnp.where(ridx == tm - 1, zn, pltpu.roll(z, tm - 1, axis=0))
    cw = convw_ref[...]
    y_conv = conv[:, GROUP_W:2 * GROUP_W] * (z_prev * cw[0:1] + z * cw[1:2] + z_next * cw[2:3])

    lane = lax.broadcasted_iota(jnp.int32, (1, GROUP_W), 1) // HEAD_DIM
    sgw = sgw_ref[...]

    def mixers(rows):
        rkv = rkv_ref[rows, :].astype(F32)
        r, k, v = rkv[:, :GROUP_W], rkv[:, GROUP_W:2 * GROUP_W], rkv[:, 2 * GROUP_W:]
        y = yf_ref[rows, :] + yb_ref[rows, :]
        mu = _seg_mean(y, seg)
        yc = y - mu
        var = _mm((yc * yc).astype(BF16), seg) * (1.0 / HEAD_DIM)
        yn = yc * lax.rsqrt(var + GN_EPS) * lnw_ref[...] + lnb_ref[...]
        bonus = _mm((r * k * rk_ref[...]).astype(BF16), seg) * v
        gate = _dot(_sigmoid(lora_ref[rows, :]), gup_ref[...])
        y_rwkv = (yn + bonus) * gate

        sg = sgu_ref[rows, :].astype(F32)
        u = jax.nn.gelu(sg[:, :GROUP_W])
        gv = jax.nn.gelu(sg[:, GROUP_W:])
        gmu = jnp.mean(gv, axis=-1, keepdims=True)
        gc = gv - gmu
        gvn = gc * lax.rsqrt(jnp.mean(gc * gc, axis=-1, keepdims=True) + NORM_EPS) * sgn_ref[...]
        mixed = []
        for ci in range((rows.stop - rows.start) // SGU_CHUNK):
            res = _dot(sgw, gvn[ci * SGU_CHUNK:(ci + 1) * SGU_CHUNK])
            m = res[0:SGU_CHUNK]
            for h in range(1, HEADS):
                m = jnp.where(lane == h, res[h * SGU_CHUNK:(h + 1) * SGU_CHUNK], m)
            mixed.append(m + sgb_ref[...])
        y_sgu = u * jnp.concatenate(mixed, axis=0)

        y_na = na_ref[rows, :].astype(F32)
        merged = jnp.concatenate([_rms(y_conv[rows]), _rms(y_rwkv), _rms(y_na), _rms(y_sgu)], axis=1)
        return (merged * mg_ref[...]).astype(BF16)

    half = tm // 2
    for r0 in (0, half):
        rows = slice(r0, r0 + half)
        out = jnp.dot(mixers(rows), wout_ref[...], preferred_element_type=F32)
        o_ref[rows, :] = x_ref[rows, :] + _rms(out) * gpost_ref[...]


def _merge(x2, conv, rkv, lora, yf, yb, y_na, sgu, batch, seg, convw, gup, rk, lnw, lnb, sgn, sgw, sgb,
           mg, wout, gpost):
    n = x2.shape[0]
    tm = ROW_TILE
    tiles_per_seq = n // batch // tm
    nh = n // 16
    row = lambda wdt: pl.BlockSpec((tm, wdt), lambda i: (i, 0))
    prev = pl.BlockSpec((16, 3 * GROUP_W), lambda i: (jnp.maximum(i * (tm // 16) - 1, 0), 0))
    nxt = pl.BlockSpec((16, 3 * GROUP_W), lambda i: (jnp.minimum((i + 1) * (tm // 16), nh - 1), 0))
    params = (seg, convw, gup, rk, lnw, lnb, sgn, sgw, sgb, mg, wout, gpost)
    return pl.pallas_call(
        functools.partial(_merge_body, tiles_per_seq=tiles_per_seq),
        grid=(n // tm,),
        in_specs=[row(D_MODEL), row(3 * GROUP_W), prev, nxt, row(3 * GROUP_W), row(LORA_W),
                  row(GROUP_W), row(GROUP_W), row(GROUP_W), row(2 * GROUP_W)]
                 + [_full(p.shape) for p in params],
        out_specs=row(D_MODEL),
        out_shape=jax.ShapeDtypeStruct((n, D_MODEL), F32),
        compiler_params=pltpu.CompilerParams(dimension_semantics=("parallel",),
                                             vmem_limit_bytes=VMEM_LIMIT),
        name="merge",
    )(x2, conv, conv, conv, rkv, lora, yf, yb, y_na, sgu, *params)


FFN_SLABS = ((0, 768), (768, 768), (1536, 768), (2304, 512))
assert sum(w for _, w in FFN_SLABS) == D_FF
HALO = 8


def _ffn_body(x_ref, xprev_ref, xnext_ref, gpre_ref, wup_ref, cw_ref, wdn_ref, gpost_ref, o_ref, *,
              tiles_per_seq):
    tm = ROW_TILE
    i = pl.program_id(0)
    first = (i % tiles_per_seq) == 0
    last = (i % tiles_per_seq) == tiles_per_seq - 1
    x = x_ref[...]
    g = gpre_ref[...]
    hf = _rms(x) * g
    hp = jnp.where(first, 0.0, _rms(xprev_ref[...]) * g)
    hn = jnp.where(last, 0.0, _rms(xnext_ref[...]) * g)
    h = hf.astype(BF16)
    hext = jnp.concatenate([hp, hf, hn], axis=0).astype(BF16)
    ext = tm + 2 * HALO

    def up(s):
        c0, w = FFN_SLABS[s]
        gate = jnp.dot(hext, wup_ref[:, c0:c0 + w], preferred_element_type=F32)
        lin = jnp.dot(h, wup_ref[:, D_FF + c0:D_FF + c0 + w], preferred_element_type=F32)
        return gate, lin

    def down(s, gate, lin):
        c0, w = FFN_SLABS[s]
        cw = cw_ref[:, c0:c0 + w]
        g_prev = pltpu.roll(gate, 1, axis=0)[HALO:HALO + tm]
        g_next = pltpu.roll(gate, ext - 1, axis=0)[HALO:HALO + tm]
        cv = g_prev * cw[0:1] + gate[HALO:HALO + tm] * cw[1:2] + g_next * cw[2:3]
        hid = jax.nn.gelu(cv) * lin
        return jnp.dot(hid.astype(BF16), wdn_ref[c0:c0 + w, :], preferred_element_type=F32)

    acc = None
    pending = up(0)
    for s in range(len(FFN_SLABS)):
        nxt = up(s + 1) if s + 1 < len(FFN_SLABS) else None
        part = down(s, *pending)
        acc = part if acc is None else acc + part
        pending = nxt
    o_ref[...] = x + _rms(acc) * gpost_ref[...]


def _ffn(x2, batch, gpre, wup, cw, wdn, gpost):
    n = x2.shape[0]
    tm = ROW_TILE
    tiles_per_seq = n // batch // tm
    nh = n // HALO
    row = pl.BlockSpec((tm, D_MODEL), lambda i: (i, 0))
    prev = pl.BlockSpec((HALO, D_MODEL), lambda i: (jnp.maximum(i * (tm // HALO) - 1, 0), 0))
    nxt = pl.BlockSpec((HALO, D_MODEL), lambda i: (jnp.minimum((i + 1) * (tm // HALO), nh - 1), 0))
    params = (gpre, wup, cw, wdn, gpost)
    return pl.pallas_call(
        functools.partial(_ffn_body, tiles_per_seq=tiles_per_seq),
        grid=(n // tm,),
        in_specs=[row, prev, nxt] + [_full(p.shape) for p in params],
        out_specs=row,
        out_shape=jax.ShapeDtypeStruct((n, D_MODEL), F32),
        compiler_params=pltpu.CompilerParams(dimension_semantics=("parallel",),
                                             vmem_limit_bytes=VMEM_LIMIT),
        name="convffn",
    )(x2, x2, x2, *params)


def _regroup_w_in(w):
    g = GROUP_W
    lora0 = 6 * g
    lora1 = lora0 + 2 * DECAY_RANK + 2 * ICLR_RANK + GATE_RANK
    pad = jnp.zeros((w.shape[0], LORA_W - (lora1 - lora0)), w.dtype)
    return jnp.concatenate([w[:, :lora0], w[:, lora1:lora1 + 5 * g], w[:, lora0:lora1], pad], axis=1).astype(BF16)


def _pad_rows(w, start):
    return jnp.zeros((LORA_W, w.shape[-1]), w.dtype).at[start:start + w.shape[0]].set(w)


def _layer_params(l, p, rows):
    both = lambda w, start: jnp.concatenate(
        [_pad_rows(w[d], start + d * w.shape[1]) for d in range(2)], axis=1)
    wd = both(p["rwkv_w_up"][l], 0)[:DECAY_LANES]
    wd_hi = wd.astype(BF16)
    wd_lo = (wd - wd_hi.astype(F32)).astype(BF16)
    wup = jnp.concatenate([wd_hi, wd_lo, wd_hi, jnp.zeros_like(wd_hi)], axis=0)
    aup = both(p["rwkv_a_up"][l], 2 * DECAY_RANK).astype(BF16)
    gup = _pad_rows(p["rwkv_g_up"][l], 2 * DECAY_RANK + 2 * ICLR_RANK).astype(BF16)
    row2 = lambda a: a.reshape(1, -1)
    sgb = jnp.repeat(p["sgu_b"][l].T, HEAD_DIM, axis=1)
    return dict(
        w_in=_regroup_w_in(p["w_in"][l]), g_mix_pre=row2(p["norm_mix_pre"][l]),
        w0=row2(p["rwkv_w0"][l]), wup=wup, a0=row2(p["rwkv_a0"][l]), aup=aup,
        k_k=row2(p["rwkv_k_k"][l]), k_a=row2(p["rwkv_k_a"][l]),
        bias=_na_bias_table(p["na_rpb"][l], rows),
        convw=p["conv_a_w"][l], gup=gup, rk=row2(p["rwkv_r_k"][l]), lnw=row2(p["rwkv_lnx_w"][l]),
        lnb=row2(p["rwkv_lnx_b"][l]), sgn=row2(p["sgu_norm"][l]),
        sgw=p["sgu_w"][l].reshape(HEADS * SGU_CHUNK, SGU_CHUNK).astype(BF16), sgb=sgb,
        mg=row2(p["merge_gain"][l]), wout=p["w_out"][l].astype(BF16), g_mix_post=row2(p["norm_mix_post"][l]),
        g_ffn_pre=row2(p["norm_ffn_pre"][l]), ffn_up=p["ffn_w_up"][l].astype(BF16), ffn_conv=p["ffn_conv"][l],
        ffn_down=p["ffn_w_down"][l].astype(BF16), g_ffn_post=row2(p["norm_ffn_post"][l]),
    )


def _constants():
    i = jnp.arange(WKV_CHUNK)
    tri = jnp.stack([i[None, :] <= i[:, None], i[None, :] >= i[:, None]]).astype(BF16)
    g = jnp.arange(GROUP_W) // HEAD_DIM
    seg = (g[:, None] == g[None, :]).astype(BF16)
    return tri, seg


def _forward(x, p):
    batch, t, _ = x.shape
    assert t % ROW_TILE == 0 and t % (NA_ROWS * GRID_W) == 0 and t // GRID_W >= NA_KH
    assert (batch * t) % INPROJ_TILE == 0 and t % WKV_BLOCK == 0
    x2 = x.reshape(batch * t, D_MODEL)
    tri, seg = _constants()
    for l in range(p["w_in"].shape[0]):
        q = _layer_params(l, p, t // GRID_W)
        conv, rkvk, na, sgu, lora, lw, nalr = _inproj(x2, q["g_mix_pre"], q["w_in"], seg, q["k_k"], q["w0"],
                                                      q["wup"], q["a0"], q["aup"])
        yf, yb = _wkv(rkvk, lw, nalr, batch, tri, q["k_a"])
        y_na = _na(na, q["bias"], batch)
        x2 = _merge(x2, conv, rkvk, lora, yf, yb, y_na, sgu, batch, seg, q["convw"], q["gup"], q["rk"],
                    q["lnw"], q["lnb"], q["sgn"], q["sgw"], q["sgb"], q["mg"], q["wout"], q["g_mix_post"])
        x2 = _ffn(x2, batch, q["g_ffn_pre"], q["ffn_up"], q["ffn_conv"], q["ffn_down"], q["g_ffn_post"])
    return x2.reshape(batch, t, D_MODEL)


def kernel(x, norm_mix_pre, norm_mix_post, norm_ffn_pre, norm_ffn_post, w_in, conv_a_w, rwkv_w0, rwkv_w_up,
           rwkv_a0, rwkv_a_up, rwkv_g_up, rwkv_k_k, rwkv_k_a, rwkv_r_k, rwkv_lnx_w, rwkv_lnx_b, na_rpb,
           sgu_norm, sgu_w, sgu_b, merge_gain, w_out, ffn_w_up, ffn_conv, ffn_w_down):
    p = dict(norm_mix_pre=norm_mix_pre, norm_mix_post=norm_mix_post, norm_ffn_pre=norm_ffn_pre,
             norm_ffn_post=norm_ffn_post, w_in=w_in, conv_a_w=conv_a_w, rwkv_w0=rwkv_w0, rwkv_w_up=rwkv_w_up,
             rwkv_a0=rwkv_a0, rwkv_a_up=rwkv_a_up, rwkv_g_up=rwkv_g_up, rwkv_k_k=rwkv_k_k, rwkv_k_a=rwkv_k_a,
             rwkv_r_k=rwkv_r_k, rwkv_lnx_w=rwkv_lnx_w, rwkv_lnx_b=rwkv_lnx_b, na_rpb=na_rpb, sgu_norm=sgu_norm,
             sgu_w=sgu_w, sgu_b=sgu_b, merge_gain=merge_gain, w_out=w_out, ffn_w_up=ffn_w_up, ffn_conv=ffn_conv,
             ffn_w_down=ffn_w_down)
    return _forward(x, p)
```

```python
import functools
import math

import jax
import jax.numpy as jnp
from jax import lax
from jax.experimental import pallas as pl
from jax.experimental.pallas import tpu as pltpu

F32 = jnp.float32
BF16 = jnp.bfloat16

D_MODEL = 1024
GRID_W = 64
HEADS = 4
HEAD_DIM = 64
GROUP_W = HEADS * HEAD_DIM
DECAY_RANK = 16
ICLR_RANK = 16
GATE_RANK = 32
LORA_W = 128
DECAY_SCALE = math.exp(-0.5)
GN_EPS = 64e-5
NA_KH = 8
NA_KW = 16
SGU_CHUNK = 128
D_FF = 2816
NORM_EPS = 1e-6
NEG_BIG = -1e30

WKV_CHUNK = 64
WKV_BLOCK = 256
ROW_TILE = 512
NA_ROWS = 8
NA_WIN = 2 * NA_KW
NA_ALIGN = 16


def _na_blocks():
    start_of = lambda q: min(max(q - NA_KW // 2, 0), GRID_W - NA_KW)
    blocks, q0 = [], 0
    while q0 < GRID_W:
        c0 = min(start_of(q0) // NA_ALIGN * NA_ALIGN, GRID_W - NA_WIN)
        q1 = q0
        while q1 < GRID_W and c0 <= start_of(q1) and start_of(q1) + NA_KW <= c0 + NA_WIN:
            q1 += 1
        nq = (q1 - q0) // 8 * 8
        blocks.append((q0, nq, c0))
        q0 += nq
    return tuple(blocks)


NA_BLOCKS = _na_blocks()
assert all(q0 % 8 == 0 and nq % 8 == 0 for q0, nq, _ in NA_BLOCKS)
VMEM_LIMIT = 56 * 1024 * 1024


def _dot(a, b):
    return jnp.dot(a.astype(BF16), b.astype(BF16), preferred_element_type=F32)


def _split(x):
    hi = x.astype(BF16)
    lo = (x - hi.astype(F32)).astype(BF16)
    return hi, lo


def _dot_x3(a, b):
    ah, al = _split(a)
    bh, bl = _split(b)
    d = lambda p, q: jnp.dot(p, q, preferred_element_type=F32)
    return d(ah, bh) + (d(ah, bl) + d(al, bh))


def _sigmoid(x):
    return 1.0 / (1.0 + jnp.exp(-x))


def _rms(x, eps=NORM_EPS):
    return x * lax.rsqrt(jnp.mean(x * x, axis=-1, keepdims=True) + eps)


def _full(shape):
    n = len(shape)
    return pl.BlockSpec(shape, lambda *_: (0,) * n)


IN_GROUPS = (3 * GROUP_W, 3 * GROUP_W, 3 * GROUP_W, 2 * GROUP_W, LORA_W)
IN_COLS = sum(IN_GROUPS)
RKVK_W = 4 * GROUP_W


INPROJ_TILE = 1024
INPROJ_PIECE = 256
DECAY_LANES = 2 * DECAY_RANK


def _inproj_body(x_ref, g_ref, w_ref, seg_ref, kk_ref, w0_ref, wd_ref, a0_ref, wa_ref,
                 conv_ref, rkvk_ref, na_ref, sgu_ref, lora_ref, lw_ref, nalr_ref):
    seg = seg_ref[...]
    lane = lax.broadcasted_iota(jnp.int32, (1, LORA_W), 1)

    def project(rows):
        h = _rms(x_ref[rows, :]) * g_ref[...]
        p = jnp.dot(h.astype(BF16), w_ref[...], preferred_element_type=F32)
        o = 0
        for ref, wdt in zip((conv_ref, rkvk_ref, na_ref, sgu_ref, lora_ref), IN_GROUPS):
            ref[rows, 0:wdt] = p[:, o:o + wdt].astype(ref.dtype)
            o += wdt
        return rows, p[:, IN_GROUPS[0] + GROUP_W:IN_GROUPS[0] + 2 * GROUP_W], p[:, IN_COLS - LORA_W:]

    def tokenwise(rows, k, lora):
        kk = k * kk_ref[...]
        rkvk_ref[rows, 3 * GROUP_W:] = (kk * lax.rsqrt(_mm((kk * kk).astype(BF16), seg) + 1e-12)).astype(BF16)
        th = jnp.tanh(lora)
        hi = th.astype(BF16).astype(F32)
        packed = jnp.where(lane < DECAY_LANES, hi,
                           jnp.where(lane < 2 * DECAY_LANES, pltpu.roll(hi, DECAY_LANES, axis=1),
                                     jnp.where(lane < 3 * DECAY_LANES, pltpu.roll(th - hi, 2 * DECAY_LANES, axis=1),
                                               0.0)))
        lw_ref[rows, :] = -DECAY_SCALE * _sigmoid(w0_ref[...] + _mm(packed.astype(BF16), wd_ref[...]))
        nalr_ref[rows, :] = (-_sigmoid(a0_ref[...] + _mm(lora.astype(BF16), wa_ref[...]))).astype(BF16)

    pending = None
    for r0 in range(0, INPROJ_TILE, INPROJ_PIECE):
        done = project(slice(r0, r0 + INPROJ_PIECE))
        if pending is not None:
            tokenwise(*pending)
        pending = done
    tokenwise(*pending)


def _inproj(x2, gain, w, seg, k_k, w0, wup, a0, aup):
    n = x2.shape[0]
    tm = INPROJ_TILE
    row = lambda wdt: pl.BlockSpec((tm, wdt), lambda i: (i, 0))
    params = (gain, w, seg, k_k, w0, wup, a0, aup)
    widths = (IN_GROUPS[0], RKVK_W, IN_GROUPS[2], IN_GROUPS[3], IN_GROUPS[4], 2 * GROUP_W, 2 * GROUP_W)
    dtypes = (BF16, BF16, BF16, BF16, F32, F32, BF16)
    return pl.pallas_call(
        _inproj_body,
        grid=(n // tm,),
        in_specs=[row(D_MODEL)] + [_full(q.shape) for q in params],
        out_specs=[row(w_) for w_ in widths],
        out_shape=[jax.ShapeDtypeStruct((n, w_), dt) for w_, dt in zip(widths, dtypes)],
        compiler_params=pltpu.CompilerParams(dimension_semantics=("parallel",),
                                             vmem_limit_bytes=VMEM_LIMIT),
        name="inproj",
    )(x2, *params)


PAIR_W = 2 * HEAD_DIM
N_PAIRS = HEADS // 2
SOLVE_LEVELS = 6


def _mm(a, b):
    return jnp.dot(a, b, preferred_element_type=F32)


def _mm_nt(a, b):
    return lax.dot_general(a, b, (((1,), (1,)), ((), ())), preferred_element_type=F32)


def _mm_tn(a, b):
    return lax.dot_general(a, b, (((0,), (0,)), ((), ())), preferred_element_type=F32)


def _mm_x3(a, b):
    return _mm(a[0], b[0]) + (_mm(a[0], b[1]) + _mm(a[1], b[0]))


def _wkv_prep(rkvk_ref, lw_ref, nalr_ref, tri, ka_gain, d, ci):
    c = WKV_CHUNK
    pieces = {}
    rows = slice(ci * c, (ci + 1) * c)
    lw = lw_ref[rows, :]
    lw_hi, lw_lo = _split(lw)
    cum = _mm(tri, lw_hi) + _mm(tri, lw_lo)
    g_end = jnp.exp(cum[c - 1:c] if d == 0 else cum[0:1])
    for p in range(N_PAIRS):
        lanes = slice(p * PAIR_W, (p + 1) * PAIR_W)
        part = lambda g: rkvk_ref[rows, g * GROUP_W + p * PAIR_W:g * GROUP_W + (p + 1) * PAIR_W]
        r, k, v, kk = part(0).astype(F32), part(1).astype(F32), part(2), part(3).astype(F32)
        nalr = nalr_ref[rows, lanes].astype(F32)
        k_eff = k * (1.0 - (1.0 + nalr) * ka_gain[:, lanes])
        b_vec = kk * nalr
        g_inc = jnp.exp(cum[:, lanes])
        g_inv = 1.0 / g_inc
        g_hat = g_end[:, lanes] * g_inv
        pieces[ci, p] = dict(
            a_t=kk * jnp.exp(cum[:, lanes] - lw[:, lanes]), r_t=r * g_inc,
            b_t=(b_vec * g_inv).astype(BF16), k_t=(k_eff * g_inv).astype(BF16),
            b_h=(b_vec * g_hat).astype(BF16), k_h=(k_eff * g_hat).astype(BF16),
            v=v, g_end=g_end[:, lanes])
    return pieces


def _wkv_body(rkvk_f, lw_f, nalr_f, rkvk_b, lw_b, nalr_b, tri_ref, ka_ref, yf_ref, yb_ref, h_ref):
    @pl.when(pl.program_id(1) == 0)
    def _():
        h_ref[...] = jnp.zeros_like(h_ref)

    c = WKV_CHUNK
    nch = WKV_BLOCK // c
    in_refs = ((rkvk_f, lw_f, nalr_f), (rkvk_b, lw_b, nalr_b))
    y_refs = (yf_ref, yb_ref)

    def head_lanes(pairs_wide):
        lane = lax.broadcasted_iota(jnp.int32, (1, pairs_wide * PAIR_W), 1)
        return [(lane % PAIR_W) // HEAD_DIM == h for h in range(2)]

    head1, head2 = head_lanes(1), head_lanes(2)
    row2 = lax.broadcasted_iota(jnp.int32, (PAIR_W, 2 * PAIR_W), 0)
    col2 = lax.broadcasted_iota(jnp.int32, (PAIR_W, 2 * PAIR_W), 1)
    t_idx, s_idx = row2 % c, col2 % c
    tri_mask = [(s_idx < t_idx) | ((row2 >= c) & (s_idx == t_idx)),
                (s_idx > t_idx) | ((row2 >= c) & (s_idx == t_idx))]
    row = lax.broadcasted_iota(jnp.int32, (PAIR_W, PAIR_W), 0)
    col = lax.broadcasted_iota(jnp.int32, (PAIR_W, PAIR_W), 1)
    diag = row == col
    own_block = (row // c) == (col // c)
    zeros_pair = jnp.zeros((c, PAIR_W), BF16)

    cps = [(ci, p) for ci in range(nch) for p in range(N_PAIRS)]
    sl = lambda ci, p: (slice(ci * c, (ci + 1) * c), slice(p * PAIR_W, (p + 1) * PAIR_W))
    st = [dict(q={}, a4={}, x={}, t={}, u={}, s_bf={}, maps={}) for _ in range(2)]
    eye_pair = jnp.where(lax.broadcasted_iota(jnp.int32, (c, PAIR_W), 1) % c
                         == lax.broadcasted_iota(jnp.int32, (c, PAIR_W), 0), 1.0, 0.0)

    def prep_stage(d, ci):
        rkvk_ref, lw_ref, nalr_ref = in_refs[d]
        st[d]["q"].update(_wkv_prep(rkvk_ref, lw_ref, nalr_ref, tri_ref[d], ka_ref[...], d, ci))

    def score_stage(d, todo):
        for ci, p in todo:
            q = st[d]["q"][ci, p]
            lhs = jnp.concatenate([q["a_t"], q["r_t"]], axis=0).astype(BF16)
            b_t, k_t = q["b_t"], q["k_t"]
            rhs = jnp.concatenate([jnp.where(head1[0], jnp.concatenate([b_t, k_t], axis=0), 0.0),
                                   jnp.where(head1[1], jnp.concatenate([k_t, b_t], axis=0), 0.0)], axis=0)
            st[d]["a4"][ci, p] = jnp.where(tri_mask[d], _mm_nt(lhs, rhs), 0.0).astype(BF16)

    def rhs_stage(d, todo):
        for ci, p in todo:
            q = st[d]["q"][ci, p]
            top = st[d]["a4"][ci, p][:c]
            v = [jnp.where(head1[h], q["v"], 0.0) for h in range(2)]
            akv = _mm(top, jnp.concatenate([zeros_pair, v[0], v[1], zeros_pair], axis=0))
            st[d]["x"][ci, p] = jnp.concatenate([q["a_t"].astype(BF16), akv.astype(BF16)], axis=1)
            st[d]["s_bf"][ci, p] = jnp.where(head1[0], top[:, :PAIR_W], top[:, PAIR_W:])
            st[d]["t"][ci, p] = eye_pair

    def by_head(cat, heads):
        return jnp.concatenate([jnp.where(heads[0], cat, 0.0), jnp.where(heads[1], cat, 0.0)], axis=0)

    def solve_level(d, j):
        s_bf, t = st[d]["s_bf"], st[d]["t"]
        last = j == SOLVE_LEVELS - 1
        for key in cps:
            t_bf = t[key].astype(BF16)
            if last:
                t[key] = t[key] + _mm(s_bf[key], by_head(t_bf, head1))
            else:
                out = _mm(s_bf[key], by_head(jnp.concatenate([s_bf[key], t_bf], axis=1), head2))
                s_bf[key] = out[:, :PAIR_W].astype(BF16)
                t[key] = t[key] + out[:, PAIR_W:]

    def apply_stage(d):
        for key in cps:
            st[d]["u"][key] = _mm(st[d]["t"][key].astype(BF16), by_head(st[d]["x"][key], head2)).astype(BF16)

    def map_stage(d, todo):
        for ci, p in todo:
            q = st[d]["q"][ci, p]
            u = st[d]["u"][ci, p]
            vz = jnp.concatenate([zeros_pair, q["v"]], axis=1)
            on = lambda h, t: jnp.where(head2[h], t, 0.0)
            ry = _mm(st[d]["a4"][ci, p][c:], jnp.concatenate([on(0, u), on(0, vz), on(1, vz), on(1, u)], axis=0))
            pq = _mm_tn(jnp.concatenate([q["b_h"], q["k_h"]], axis=0), jnp.concatenate([u, vz], axis=0))
            rhat = q["r_t"] + ry[:, :PAIR_W]
            pmat = jnp.where(own_block, pq[:, :PAIR_W], 0.0) + jnp.where(diag, q["g_end"], 0.0)
            qmat = jnp.where(own_block, pq[:, PAIR_W:], 0.0)
            st[d]["maps"][ci, p] = (jnp.concatenate([rhat, pmat], axis=0).astype(BF16), ry[:, PAIR_W:], qmat)

    def carry_step(d, step):
        ci = step if d == 0 else nch - 1 - step
        for p in range(N_PAIRS):
            rows, lanes = sl(ci, p)
            lhs, yi, qmat = st[d]["maps"][ci, p]
            out = _mm(lhs, jnp.concatenate(_split(h_ref[d, p]), axis=1))
            out = out[:, :PAIR_W] + out[:, PAIR_W:]
            y_refs[d][rows, lanes] = out[:c] + yi
            h_ref[d, p] = out[c:] + qmat

    chunk = lambda ci: [(ci, p) for p in range(N_PAIRS)]
    for ci in range(nch):
        prep_stage(0, ci)
    for ci in range(nch):
        score_stage(0, chunk(ci))
        prep_stage(1, ci)
    for ci in range(nch):
        rhs_stage(0, chunk(ci))
        score_stage(1, chunk(ci))
    rhs_stage(1, cps)
    for j in range(SOLVE_LEVELS):
        for d in range(2):
            solve_level(d, j)
    for d in range(2):
        apply_stage(d)
    for step in range(nch):
        for d in range(2):
            map_stage(d, chunk(step if d == 0 else nch - 1 - step))
        if step:
            for d in range(2):
                carry_step(d, step - 1)
    for d in range(2):
        carry_step(d, nch - 1)


def _wkv(rkvk, lw, nalr, batch, tri, k_a):
    n = rkvk.shape[0]
    nb = n // batch // WKV_BLOCK
    fwd = lambda wdt, col=0: pl.BlockSpec((WKV_BLOCK, wdt), lambda b, j: (b * nb + j, col))
    bwd = lambda wdt, col=0: pl.BlockSpec((WKV_BLOCK, wdt), lambda b, j: (b * nb + nb - 1 - j, col))
    return pl.pallas_call(
        _wkv_body,
        grid=(batch, nb),
        in_specs=[fwd(RKVK_W), fwd(GROUP_W, 0), fwd(GROUP_W, 0), bwd(RKVK_W), bwd(GROUP_W, 1), bwd(GROUP_W, 1),
                  _full(tri.shape), _full(k_a.shape)],
        out_specs=[fwd(GROUP_W), bwd(GROUP_W)],
        out_shape=[jax.ShapeDtypeStruct((n, GROUP_W), F32)] * 2,
        scratch_shapes=[pltpu.VMEM((2, N_PAIRS, PAIR_W, PAIR_W), F32)],
        compiler_params=pltpu.CompilerParams(dimension_semantics=("parallel", "arbitrary"),
                                             vmem_limit_bytes=VMEM_LIMIT),
        name="wkv7",
    )(rkvk, lw, nalr, rkvk, lw, nalr, tri, k_a)


def _na_body(q_ref, k_ref, v_ref, bias_ref, o_ref, *, rows):
    j = pl.program_id(1)
    lane = lax.broadcasted_iota(jnp.int32, (1, GROUP_W), 1) // HEAD_DIM

    def window(ref, start, c0):
        return jnp.concatenate([ref[pl.ds(start + i * GRID_W + c0, NA_WIN), :] for i in range(NA_KH)], axis=0)

    work = []
    for rr in range(NA_ROWS):
        r = j * NA_ROWS + rr
        rs = jnp.clip(r - NA_KH // 2, 0, rows - NA_KH)
        start = pl.multiple_of(rs * GRID_W, GRID_W)
        q = q_ref[rr * GRID_W:(rr + 1) * GRID_W, :].astype(F32) * (HEAD_DIM ** -0.5)
        bias = bias_ref[rs - r + (NA_KH - 1)]
        off = 0
        for q0, nq, c0 in NA_BLOCKS:
            qb = q[q0:q0 + nq]
            qs = jnp.concatenate([jnp.where(lane == h, qb, 0.0) for h in range(HEADS)], axis=0).astype(BF16)
            s = _mm_nt(qs, window(k_ref, start, c0)) + bias[off:off + HEADS * nq]
            work.append((start, nq, c0, s))
            off += HEADS * nq
    soft = []
    for start, nq, c0, s in work:
        p = jnp.exp(s - jnp.max(s, axis=-1, keepdims=True))
        soft.append((p.astype(BF16), jnp.sum(p, axis=-1, keepdims=True)))
    outs = []
    for (start, nq, c0, _), (p, norm) in zip(work, soft):
        o = _mm(p, window(v_ref, start, c0)) / norm
        acc = o[0:nq]
        for h in range(1, HEADS):
            acc = jnp.where(lane == h, o[h * nq:(h + 1) * nq], acc)
        outs.append(acc)
    per_row = len(NA_BLOCKS)
    for rr in range(NA_ROWS):
        o_ref[rr * GRID_W:(rr + 1) * GRID_W, :] = jnp.concatenate(
            outs[rr * per_row:(rr + 1) * per_row], axis=0).astype(o_ref.dtype)


def _na(na, bias, batch):
    n = na.shape[0]
    t = n // batch
    rows = t // GRID_W
    nj = rows // NA_ROWS
    blk = NA_ROWS * GRID_W
    return pl.pallas_call(
        functools.partial(_na_body, rows=rows),
        grid=(batch, nj),
        in_specs=[pl.BlockSpec((blk, GROUP_W), lambda b, j: (b * nj + j, 0)),
                  pl.BlockSpec((t, GROUP_W), lambda b, j: (b, 1)),
                  pl.BlockSpec((t, GROUP_W), lambda b, j: (b, 2)),
                  _full(bias.shape)],
        out_specs=pl.BlockSpec((blk, GROUP_W), lambda b, j: (b * nj + j, 0)),
        out_shape=jax.ShapeDtypeStruct((n, GROUP_W), BF16),
        compiler_params=pltpu.CompilerParams(dimension_semantics=("parallel", "arbitrary"),
                                             vmem_limit_bytes=VMEM_LIMIT),
        name="nattn",
    )(na, na, na, bias)


def _na_bias_table(rpb, rows):
    kh = min(NA_KH, rows)
    c = jnp.arange(GRID_W)
    col_start = jnp.clip(c - NA_KW // 2, 0, GRID_W - NA_KW)
    col_mask = (c[None, :] >= col_start[:, None]) & (c[None, :] < col_start[:, None] + NA_KW)
    dx = jnp.clip(c[None, :] - c[:, None], -(NA_KW - 1), NA_KW - 1) + (NA_KW - 1)
    onehot = (dx[None] == jnp.arange(2 * NA_KW - 1)[:, None, None]).astype(F32)
    by_dy = jnp.einsum("hyx,xqw->hyqw", rpb, onehot, precision=lax.Precision.HIGHEST)
    by_dy = jnp.where(col_mask[None, None], by_dy, NEG_BIG)
    b = jnp.stack([by_dy[:, delta:delta + kh] for delta in range(NA_KH)], axis=0)
    b = jnp.transpose(b, (0, 1, 3, 2, 4))
    blocks = [b[:, :, q0:q0 + nq, :, c0:c0 + NA_WIN].reshape(NA_KH, HEADS * nq, kh * NA_WIN)
              for q0, nq, c0 in NA_BLOCKS]
    return jnp.concatenate(blocks, axis=1).astype(F32)


def _seg_sum(x, seg):
    hi, lo = _split(x)
    return _mm(hi, seg) + _mm(lo, seg)


def _seg_mean(x, seg):
    return _seg_sum(x, seg) * (1.0 / HEAD_DIM)


def _merge_body(x_ref, conv_ref, cprev_ref, cnext_ref, rkv_ref, lora_ref, yf_ref, yb_ref, na_ref, sgu_ref,
                seg_ref, convw_ref, gup_ref, rk_ref, lnw_ref, lnb_ref, sgn_ref, sgw_ref, sgb_ref,
                mg_ref, wout_ref, gpost_ref, o_ref, *, tiles_per_seq):
    tm = ROW_TILE
    i = pl.program_id(0)
    seg = seg_ref[...]

    conv = conv_ref[...].astype(F32)
    z = conv[:, 2 * GROUP_W:] * conv[:, :GROUP_W]
    zp = cprev_ref[15:16, :].astype(F32)
    zp = zp[:, 2 * GROUP_W:] * zp[:, :GROUP_W]
    zn = cnext_ref[0:1, :].astype(F32)
    zn = zn[:, 2 * GROUP_W:] * zn[:, :GROUP_W]
    first = (i % tiles_per_seq) == 0
    last = (i % tiles_per_seq) == tiles_per_seq - 1
    zp = jnp.where(first, 0.0, zp)
    zn = jnp.where(last, 0.0, zn)
    ridx = lax.broadcasted_iota(jnp.int32, (tm, 1), 0)
    z_prev = jnp.where(ridx == 0, zp, pltpu.roll(z, 1, axis=0))
    z_next = jnp.where(ridx == tm - 1, zn, pltpu.roll(z, tm - 1, axis=0))
    cw = convw_ref[...]
    y_conv = conv[:, GROUP_W:2 * GROUP_W] * (z_prev * cw[0:1] + z * cw[1:2] + z_next * cw[2:3])

    lane = lax.broadcasted_iota(jnp.int32, (1, GROUP_W), 1) // HEAD_DIM
    sgw = sgw_ref[...]

    def mixers(rows):
        rkv = rkv_ref[rows, :].astype(F32)
        r, k, v = rkv[:, :GROUP_W], rkv[:, GROUP_W:2 * GROUP_W], rkv[:, 2 * GROUP_W:]
        y = yf_ref[rows, :] + yb_ref[rows, :]
        mu = _seg_mean(y, seg)
        yc = y - mu
        var = _mm((yc * yc).astype(BF16), seg) * (1.0 / HEAD_DIM)
        yn = yc * lax.rsqrt(var + GN_EPS) * lnw_ref[...] + lnb_ref[...]
        bonus = _mm((r * k * rk_ref[...]).astype(BF16), seg) * v
        gate = _dot(_sigmoid(lora_ref[rows, :]), gup_ref[...])
        y_rwkv = (yn + bonus) * gate

        sg = sgu_ref[rows, :].astype(F32)
        u = jax.nn.gelu(sg[:, :GROUP_W])
        gv = jax.nn.gelu(sg[:, GROUP_W:])
        gmu = jnp.mean(gv, axis=-1, keepdims=True)
        gc = gv - gmu
        gvn = gc * lax.rsqrt(jnp.mean(gc * gc, axis=-1, keepdims=True) + NORM_EPS) * sgn_ref[...]
        mixed = []
        for ci in range((rows.stop - rows.start) // SGU_CHUNK):
            res = _dot(sgw, gvn[ci * SGU_CHUNK:(ci + 1) * SGU_CHUNK])
            m = res[0:SGU_CHUNK]
            for h in range(1, HEADS):
                m = jnp.where(lane == h, res[h * SGU_CHUNK:(h + 1) * SGU_CHUNK], m)
            mixed.append(m + sgb_ref[...])
        y_sgu = u * jnp.concatenate(mixed, axis=0)

        y_na = na_ref[rows, :].astype(F32)
        merged = jnp.concatenate([_rms(y_conv[rows]), _rms(y_rwkv), _rms(y_na), _rms(y_sgu)], axis=1)
        return (merged * mg_ref[...]).astype(BF16)

    half = tm // 2
    for r0 in (0, half):
        rows = slice(r0, r0 + half)
        out = jnp.dot(mixers(rows), wout_ref[...], preferred_element_type=F32)
        o_ref[rows, :] = x_ref[rows, :] + _rms(out) * gpost_ref[...]


def _merge(x2, conv, rkv, lora, yf, yb, y_na, sgu, batch, seg, convw, gup, rk, lnw, lnb, sgn, sgw, sgb,
           mg, wout, gpost):
    n = x2.shape[0]
    tm = ROW_TILE
    tiles_per_seq = n // batch // tm
    nh = n // 16
    row = lambda wdt: pl.BlockSpec((tm, wdt), lambda i: (i, 0))
    prev = pl.BlockSpec((16, 3 * GROUP_W), lambda i: (jnp.maximum(i * (tm // 16) - 1, 0), 0))
    nxt = pl.BlockSpec((16, 3 * GROUP_W), lambda i: (jnp.minimum((i + 1) * (tm // 16), nh - 1), 0))
    params = (seg, convw, gup, rk, lnw, lnb, sgn, sgw, sgb, mg, wout, gpost)
    return pl.pallas_call(
        functools.partial(_merge_body, tiles_per_seq=tiles_per_seq),
        grid=(n // tm,),
        in_specs=[row(D_MODEL), row(3 * GROUP_W), prev, nxt, row(3 * GROUP_W), row(LORA_W),
                  row(GROUP_W), row(GROUP_W), row(GROUP_W), row(2 * GROUP_W)]
                 + [_full(p.shape) for p in params],
        out_specs=row(D_MODEL),
        out_shape=jax.ShapeDtypeStruct((n, D_MODEL), F32),
        compiler_params=pltpu.CompilerParams(dimension_semantics=("parallel",),
                                             vmem_limit_bytes=VMEM_LIMIT),
        name="merge",
    )(x2, conv, conv, conv, rkv, lora, yf, yb, y_na, sgu, *params)


FFN_SLABS = ((0, 768), (768, 768), (1536, 768), (2304, 512))
assert sum(w for _, w in FFN_SLABS) == D_FF
HALO = 8


def _ffn_body(x_ref, xprev_ref, xnext_ref, gpre_ref, wup_ref, cw_ref, wdn_ref, gpost_ref, o_ref, *,
              tiles_per_seq):
    tm = ROW_TILE
    i = pl.program_id(0)
    first = (i % tiles_per_seq) == 0
    last = (i % tiles_per_seq) == tiles_per_seq - 1
    x = x_ref[...]
    g = gpre_ref[...]
    hf = _rms(x) * g
    hp = jnp.where(first, 0.0, _rms(xprev_ref[...]) * g)
    hn = jnp.where(last, 0.0, _rms(xnext_ref[...]) * g)
    h = hf.astype(BF16)
    hext = jnp.concatenate([hp, hf, hn], axis=0).astype(BF16)
    ext = tm + 2 * HALO

    def up(s):
        c0, w = FFN_SLABS[s]
        gate = jnp.dot(hext, wup_ref[:, c0:c0 + w], preferred_element_type=F32)
        lin = jnp.dot(h, wup_ref[:, D_FF + c0:D_FF + c0 + w], preferred_element_type=F32)
        return gate, lin

    def down(s, gate, lin):
        c0, w = FFN_SLABS[s]
        cw = cw_ref[:, c0:c0 + w]
        g_prev = pltpu.roll(gate, 1, axis=0)[HALO:HALO + tm]
        g_next = pltpu.roll(gate, ext - 1, axis=0)[HALO:HALO + tm]
        cv = g_prev * cw[0:1] + gate[HALO:HALO + tm] * cw[1:2] + g_next * cw[2:3]
        hid = jax.nn.gelu(cv) * lin
        return jnp.dot(hid.astype(BF16), wdn_ref[c0:c0 + w, :], preferred_element_type=F32)

    acc = None
    pending = up(0)
    for s in range(len(FFN_SLABS)):
        nxt = up(s + 1) if s + 1 < len(FFN_SLABS) else None
        part = down(s, *pending)
        acc = part if acc is None else acc + part
        pending = nxt
    o_ref[...] = x + _rms(acc) * gpost_ref[...]


def _ffn(x2, batch, gpre, wup, cw, wdn, gpost):
    n = x2.shape[0]
    tm = ROW_TILE
    tiles_per_seq = n // batch // tm
    nh = n // HALO
    row = pl.BlockSpec((tm, D_MODEL), lambda i: (i, 0))
    prev = pl.BlockSpec((HALO, D_MODEL), lambda i: (jnp.maximum(i * (tm // HALO) - 1, 0), 0))
    nxt = pl.BlockSpec((HALO, D_MODEL), lambda i: (jnp.minimum((i + 1) * (tm // HALO), nh - 1), 0))
    params = (gpre, wup, cw, wdn, gpost)
    return pl.pallas_call(
        functools.partial(_ffn_body, tiles_per_seq=tiles_per_seq),
        grid=(n // tm,),
        in_specs=[row, prev, nxt] + [_full(p.shape) for p in params],
        out_specs=row,
        out_shape=jax.ShapeDtypeStruct((n, D_MODEL), F32),
        compiler_params=pltpu.CompilerParams(dimension_semantics=("parallel",),
                                             vmem_limit_bytes=VMEM_LIMIT),
        name="convffn",
    )(x2, x2, x2, *params)


def _regroup_w_in(w):
    g = GROUP_W
    lora0 = 6 * g
    lora1 = lora0 + 2 * DECAY_RANK + 2 * ICLR_RANK + GATE_RANK
    pad = jnp.zeros((w.shape[0], LORA_W - (lora1 - lora0)), w.dtype)
    return jnp.concatenate([w[:, :lora0], w[:, lora1:lora1 + 5 * g], w[:, lora0:lora1], pad], axis=1).astype(BF16)


def _pad_rows(w, start):
    return jnp.zeros((LORA_W, w.shape[-1]), w.dtype).at[start:start + w.shape[0]].set(w)


def _layer_params(l, p, rows):
    both = lambda w, start: jnp.concatenate(
        [_pad_rows(w[d], start + d * w.shape[1]) for d in range(2)], axis=1)
    wd = both(p["rwkv_w_up"][l], 0)[:DECAY_LANES]
    wd_hi = wd.astype(BF16)
    wd_lo = (wd - wd_hi.astype(F32)).astype(BF16)
    wup = jnp.concatenate([wd_hi, wd_lo, wd_hi, jnp.zeros_like(wd_hi)], axis=0)
    aup = both(p["rwkv_a_up"][l], 2 * DECAY_RANK).astype(BF16)
    gup = _pad_rows(p["rwkv_g_up"][l], 2 * DECAY_RANK + 2 * ICLR_RANK).astype(BF16)
    row2 = lambda a: a.reshape(1, -1)
    sgb = jnp.repeat(p["sgu_b"][l].T, HEAD_DIM, axis=1)
    return dict(
        w_in=_regroup_w_in(p["w_in"][l]), g_mix_pre=row2(p["norm_mix_pre"][l]),
        w0=row2(p["rwkv_w0"][l]), wup=wup, a0=row2(p["rwkv_a0"][l]), aup=aup,
        k_k=row2(p["rwkv_k_k"][l]), k_a=row2(p["rwkv_k_a"][l]),
        bias=_na_bias_table(p["na_rpb"][l], rows),
        convw=p["conv_a_w"][l], gup=gup, rk=row2(p["rwkv_r_k"][l]), lnw=row2(p["rwkv_lnx_w"][l]),
        lnb=row2(p["rwkv_lnx_b"][l]), sgn=row2(p["sgu_norm"][l]),
        sgw=p["sgu_w"][l].reshape(HEADS * SGU_CHUNK, SGU_CHUNK).astype(BF16), sgb=sgb,
        mg=row2(p["merge_gain"][l]), wout=p["w_out"][l].astype(BF16), g_mix_post=row2(p["norm_mix_post"][l]),
        g_ffn_pre=row2(p["norm_ffn_pre"][l]), ffn_up=p["ffn_w_up"][l].astype(BF16), ffn_conv=p["ffn_conv"][l],
        ffn_down=p["ffn_w_down"][l].astype(BF16), g_ffn_post=row2(p["norm_ffn_post"][l]),
    )


def _constants():
    i = jnp.arange(WKV_CHUNK)
    tri = jnp.stack([i[None, :] <= i[:, None], i[None, :] >= i[:, None]]).astype(BF16)
    g = jnp.arange(GROUP_W) // HEAD_DIM
    seg = (g[:, None] == g[None, :]).astype(BF16)
    return tri, seg


def _forward(x, p):
    batch, t, _ = x.shape
    assert t % ROW_TILE == 0 and t % (NA_ROWS * GRID_W) == 0 and t // GRID_W >= NA_KH
    assert (batch * t) % INPROJ_TILE == 0 and t % WKV_BLOCK == 0
    x2 = x.reshape(batch * t, D_MODEL)
    tri, seg = _constants()
    for l in range(p["w_in"].shape[0]):
        q = _layer_params(l, p, t // GRID_W)
        conv, rkvk, na, sgu, lora, lw, nalr = _inproj(x2, q["g_mix_pre"], q["w_in"], seg, q["k_k"], q["w0"],
                                                      q["wup"], q["a0"], q["aup"])
        yf, yb = _wkv(rkvk, lw, nalr, batch, tri, q["k_a"])
        y_na = _na(na, q["bias"], batch)
        x2 = _merge(x2, conv, rkvk, lora, yf, yb, y_na, sgu, batch, seg, q["convw"], q["gup"], q["rk"],
                    q["lnw"], q["lnb"], q["sgn"], q["sgw"], q["sgb"], q["mg"], q["wout"], q["g_mix_post"])
        x2 = _ffn(x2, batch, q["g_ffn_pre"], q["ffn_up"], q["ffn_conv"], q["ffn_down"], q["g_ffn_post"])
    return x2.reshape(batch, t, D_MODEL)


def kernel(x, norm_mix_pre, norm_mix_post, norm_ffn_pre, norm_ffn_post, w_in, conv_a_w, rwkv_w0, rwkv_w_up,
           rwkv_a0, rwkv_a_up, rwkv_g_up, rwkv_k_k, rwkv_k_a, rwkv_r_k, rwkv_lnx_w, rwkv_lnx_b, na_rpb,
           sgu_norm, sgu_w, sgu_b, merge_gain, w_out, ffn_w_up, ffn_conv, ffn_w_down):
    p = dict(norm_mix_pre=norm_mix_pre, norm_mix_post=norm_mix_post, norm_ffn_pre=norm_ffn_pre,
             norm_ffn_post=norm_ffn_post, w_in=w_in, conv_a_w=conv_a_w, rwkv_w0=rwkv_w0, rwkv_w_up=rwkv_w_up,
             rwkv_a0=rwkv_a0, rwkv_a_up=rwkv_a_up, rwkv_g_up=rwkv_g_up, rwkv_k_k=rwkv_k_k, rwkv_k_a=rwkv_k_a,
             rwkv_r_k=rwkv_r_k, rwkv_lnx_w=rwkv_lnx_w, rwkv_lnx_b=rwkv_lnx_b, na_rpb=na_rpb, sgu_norm=sgu_norm,
             sgu_w=sgu_w, sgu_b=sgu_b, merge_gain=merge_gain, w_out=w_out, ffn_w_up=ffn_w_up, ffn_conv=ffn_conv,
             ffn_w_down=ffn_w_down)
    return _forward(x, p)
```

```python
import functools
import math

import jax
import jax.numpy as jnp
import numpy as np
from jax import lax
from jax.experimental import pallas as pl
from jax.experimental.pallas import tpu as pltpu

F32 = jnp.float32
BF16 = jnp.bfloat16

D_MODEL = 1024
GRID_W = 64
HEADS = 4
HEAD_DIM = 64
GROUP_W = HEADS * HEAD_DIM
DECAY_RANK = 16
ICLR_RANK = 16
GATE_RANK = 32
LORA_W = 128
DECAY_SCALE = math.exp(-0.5)
GN_EPS = 64e-5
NA_KH = 8
NA_KW = 16
SGU_CHUNK = 128
D_FF = 2816
NORM_EPS = 1e-6
NEG_BIG = -1e30

WKV_CHUNK = 64
WKV_BLOCK = 256
ROW_TILE = 512
NA_ROWS = 8
NA_WIN = 2 * NA_KW
NA_ALIGN = 16


def _na_blocks():
    start_of = lambda q: min(max(q - NA_KW // 2, 0), GRID_W - NA_KW)
    blocks, q0 = [], 0
    while q0 < GRID_W:
        c0 = min(start_of(q0) // NA_ALIGN * NA_ALIGN, GRID_W - NA_WIN)
        q1 = q0
        while q1 < GRID_W and c0 <= start_of(q1) and start_of(q1) + NA_KW <= c0 + NA_WIN:
            q1 += 1
        nq = (q1 - q0) // 8 * 8
        blocks.append((q0, nq, c0))
        q0 += nq
    return tuple(blocks)


NA_BLOCKS = _na_blocks()
assert all(q0 % 8 == 0 and nq % 8 == 0 for q0, nq, _ in NA_BLOCKS)
VMEM_LIMIT = 56 * 1024 * 1024


def _dot(a, b):
    return jnp.dot(a.astype(BF16), b.astype(BF16), preferred_element_type=F32)


def _split(x):
    hi = x.astype(BF16)
    lo = (x - hi.astype(F32)).astype(BF16)
    return hi, lo


def _dot_x3(a, b):
    ah, al = _split(a)
    bh, bl = _split(b)
    d = lambda p, q: jnp.dot(p, q, preferred_element_type=F32)
    return d(ah, bh) + (d(ah, bl) + d(al, bh))


def _sigmoid(x):
    return 1.0 / (1.0 + jnp.exp(-x))


def _rms(x, eps=NORM_EPS):
    return x * lax.rsqrt(jnp.mean(x * x, axis=-1, keepdims=True) + eps)


def _full(shape):
    n = len(shape)
    return pl.BlockSpec(shape, lambda *_: (0,) * n)


def _layer(a, l):
    n = a.ndim - 1
    return pl.BlockSpec((None,) + a.shape[1:], lambda *_: (l,) + (0,) * n)


IN_GROUPS = (3 * GROUP_W, 3 * GROUP_W, 3 * GROUP_W, 2 * GROUP_W, LORA_W)
IN_COLS = sum(IN_GROUPS)
RKVK_W = 4 * GROUP_W


INPROJ_TILE = 1024
INPROJ_PIECE = 256
DECAY_LANES = 2 * DECAY_RANK


def _inproj_body(x_ref, g_ref, w_ref, seg_ref, kk_ref, w0_ref, wd_ref, a0_ref, wa_ref,
                 conv_ref, rkvk_ref, na_ref, sgu_ref, lora_ref, lw_ref, nalr_ref):
    seg = seg_ref[...]
    lane = lax.broadcasted_iota(jnp.int32, (1, LORA_W), 1)

    def project(rows):
        h = _rms(x_ref[rows, :]) * g_ref[...]
        p = jnp.dot(h.astype(BF16), w_ref[...], preferred_element_type=F32)
        o = 0
        for ref, wdt in zip((conv_ref, rkvk_ref, na_ref, sgu_ref, lora_ref), IN_GROUPS):
            ref[rows, 0:wdt] = p[:, o:o + wdt].astype(ref.dtype)
            o += wdt
        return rows, p[:, IN_GROUPS[0] + GROUP_W:IN_GROUPS[0] + 2 * GROUP_W], p[:, IN_COLS - LORA_W:]

    def tokenwise(rows, k, lora):
        kk = k * kk_ref[...]
        rkvk_ref[rows, 3 * GROUP_W:] = (kk * lax.rsqrt(_mm((kk * kk).astype(BF16), seg) + 1e-12)).astype(BF16)
        th = jnp.tanh(lora)
        hi = th.astype(BF16).astype(F32)
        packed = jnp.where(lane < DECAY_LANES, hi,
                           jnp.where(lane < 2 * DECAY_LANES, pltpu.roll(hi, DECAY_LANES, axis=1),
                                     jnp.where(lane < 3 * DECAY_LANES, pltpu.roll(th - hi, 2 * DECAY_LANES, axis=1),
                                               0.0)))
        lw_ref[rows, :] = -DECAY_SCALE * _sigmoid(w0_ref[...] + _mm(packed.astype(BF16), wd_ref[...]))
        nalr_ref[rows, :] = (-_sigmoid(a0_ref[...] + _mm(lora.astype(BF16), wa_ref[...]))).astype(BF16)

    pending = None
    for r0 in range(0, INPROJ_TILE, INPROJ_PIECE):
        done = project(slice(r0, r0 + INPROJ_PIECE))
        if pending is not None:
            tokenwise(*pending)
        pending = done
    tokenwise(*pending)


def _inproj(x2, l, gain, w, seg, k_k, w0, wup, a0, aup):
    n = x2.shape[0]
    tm = INPROJ_TILE
    row = lambda wdt: pl.BlockSpec((tm, wdt), lambda i: (i, 0))
    params = (gain, w, seg, k_k, w0, wup, a0, aup)
    specs = [_full(q.shape) if q is seg else _layer(q, l) for q in params]
    widths = (IN_GROUPS[0], RKVK_W, IN_GROUPS[2], IN_GROUPS[3], IN_GROUPS[4], 2 * GROUP_W, 2 * GROUP_W)
    dtypes = (BF16, BF16, BF16, BF16, F32, F32, BF16)
    return pl.pallas_call(
        _inproj_body,
        grid=(n // tm,),
        in_specs=[row(D_MODEL)] + specs,
        out_specs=[row(w_) for w_ in widths],
        out_shape=[jax.ShapeDtypeStruct((n, w_), dt) for w_, dt in zip(widths, dtypes)],
        compiler_params=pltpu.CompilerParams(dimension_semantics=("parallel",),
                                             vmem_limit_bytes=VMEM_LIMIT),
        name="inproj",
    )(x2, *params)


PAIR_W = 2 * HEAD_DIM
N_PAIRS = HEADS // 2
SOLVE_LEVELS = 6


def _mm(a, b):
    return jnp.dot(a, b, preferred_element_type=F32)


def _mm_nt(a, b):
    return lax.dot_general(a, b, (((1,), (1,)), ((), ())), preferred_element_type=F32)


def _mm_tn(a, b):
    return lax.dot_general(a, b, (((0,), (0,)), ((), ())), preferred_element_type=F32)


def _mm_x3(a, b):
    return _mm(a[0], b[0]) + (_mm(a[0], b[1]) + _mm(a[1], b[0]))


def _wkv_prep(rkvk_ref, lw_ref, nalr_ref, tri, ka_gain, d, ci):
    c = WKV_CHUNK
    pieces = {}
    rows = slice(ci * c, (ci + 1) * c)
    lw = lw_ref[rows, :]
    lw_hi, lw_lo = _split(lw)
    cum = _mm(tri, lw_hi) + _mm(tri, lw_lo)
    g_end = jnp.exp(cum[c - 1:c] if d == 0 else cum[0:1])
    for p in range(N_PAIRS):
        lanes = slice(p * PAIR_W, (p + 1) * PAIR_W)
        part = lambda g: rkvk_ref[rows, g * GROUP_W + p * PAIR_W:g * GROUP_W + (p + 1) * PAIR_W]
        r, k, v, kk = part(0).astype(F32), part(1).astype(F32), part(2), part(3).astype(F32)
        nalr = nalr_ref[rows, lanes].astype(F32)
        k_eff = k * (1.0 - (1.0 + nalr) * ka_gain[:, lanes])
        b_vec = kk * nalr
        g_inc = jnp.exp(cum[:, lanes])
        g_inv = 1.0 / g_inc
        g_hat = g_end[:, lanes] * g_inv
        pieces[ci, p] = dict(
            a_t=kk * jnp.exp(cum[:, lanes] - lw[:, lanes]), r_t=r * g_inc,
            b_t=(b_vec * g_inv).astype(BF16), k_t=(k_eff * g_inv).astype(BF16),
            b_h=(b_vec * g_hat).astype(BF16), k_h=(k_eff * g_hat).astype(BF16),
            v=v, g_end=g_end[:, lanes])
    return pieces


def _wkv_body(rkvk_f, lw_f, nalr_f, rkvk_b, lw_b, nalr_b, tri_ref, ka_ref, yf_ref, yb_ref, h_ref):
    @pl.when(pl.program_id(1) == 0)
    def _():
        h_ref[...] = jnp.zeros_like(h_ref)

    c = WKV_CHUNK
    nch = WKV_BLOCK // c
    in_refs = ((rkvk_f, lw_f, nalr_f), (rkvk_b, lw_b, nalr_b))
    y_refs = (yf_ref, yb_ref)

    def head_lanes(pairs_wide):
        lane = lax.broadcasted_iota(jnp.int32, (1, pairs_wide * PAIR_W), 1)
        return [(lane % PAIR_W) // HEAD_DIM == h for h in range(2)]

    head1, head2 = head_lanes(1), head_lanes(2)
    row2 = lax.broadcasted_iota(jnp.int32, (PAIR_W, 2 * PAIR_W), 0)
    col2 = lax.broadcasted_iota(jnp.int32, (PAIR_W, 2 * PAIR_W), 1)
    t_idx, s_idx = row2 % c, col2 % c
    tri_mask = [(s_idx < t_idx) | ((row2 >= c) & (s_idx == t_idx)),
                (s_idx > t_idx) | ((row2 >= c) & (s_idx == t_idx))]
    row = lax.broadcasted_iota(jnp.int32, (PAIR_W, PAIR_W), 0)
    col = lax.broadcasted_iota(jnp.int32, (PAIR_W, PAIR_W), 1)
    diag = row == col
    own_block = (row // c) == (col // c)
    zeros_pair = jnp.zeros((c, PAIR_W), BF16)

    cps = [(ci, p) for ci in range(nch) for p in range(N_PAIRS)]
    sl = lambda ci, p: (slice(ci * c, (ci + 1) * c), slice(p * PAIR_W, (p + 1) * PAIR_W))
    st = [dict(q={}, a4={}, x={}, t={}, u={}, s_bf={}, maps={}) for _ in range(2)]
    eye_pair = jnp.where(lax.broadcasted_iota(jnp.int32, (c, PAIR_W), 1) % c
                         == lax.broadcasted_iota(jnp.int32, (c, PAIR_W), 0), 1.0, 0.0)

    def prep_stage(d, ci):
        rkvk_ref, lw_ref, nalr_ref = in_refs[d]
        st[d]["q"].update(_wkv_prep(rkvk_ref, lw_ref, nalr_ref, tri_ref[d], ka_ref[...], d, ci))

    def score_stage(d, todo):
        for ci, p in todo:
            q = st[d]["q"][ci, p]
            lhs = jnp.concatenate([q["a_t"], q["r_t"]], axis=0).astype(BF16)
            b_t, k_t = q["b_t"], q["k_t"]
            rhs = jnp.concatenate([jnp.where(head1[0], jnp.concatenate([b_t, k_t], axis=0), 0.0),
                                   jnp.where(head1[1], jnp.concatenate([k_t, b_t], axis=0), 0.0)], axis=0)
            st[d]["a4"][ci, p] = jnp.where(tri_mask[d], _mm_nt(lhs, rhs), 0.0).astype(BF16)

    def rhs_stage(d, todo):
        for ci, p in todo:
            q = st[d]["q"][ci, p]
            top = st[d]["a4"][ci, p][:c]
            v = [jnp.where(head1[h], q["v"], 0.0) for h in range(2)]
            akv = _mm(top, jnp.concatenate([zeros_pair, v[0], v[1], zeros_pair], axis=0))
            st[d]["x"][ci, p] = jnp.concatenate([q["a_t"].astype(BF16), akv.astype(BF16)], axis=1)
            st[d]["s_bf"][ci, p] = jnp.where(head1[0], top[:, :PAIR_W], top[:, PAIR_W:])
            st[d]["t"][ci, p] = eye_pair

    def by_head(cat, heads):
        return jnp.concatenate([jnp.where(heads[0], cat, 0.0), jnp.where(heads[1], cat, 0.0)], axis=0)

    def solve_level(d, j):
        s_bf, t = st[d]["s_bf"], st[d]["t"]
        last = j == SOLVE_LEVELS - 1
        for key in cps:
            t_bf = t[key].astype(BF16)
            if last:
                t[key] = t[key] + _mm(s_bf[key], by_head(t_bf, head1))
            else:
                out = _mm(s_bf[key], by_head(jnp.concatenate([s_bf[key], t_bf], axis=1), head2))
                s_bf[key] = out[:, :PAIR_W].astype(BF16)
                t[key] = t[key] + out[:, PAIR_W:]

    def apply_stage(d):
        for key in cps:
            st[d]["u"][key] = _mm(st[d]["t"][key].astype(BF16), by_head(st[d]["x"][key], head2)).astype(BF16)

    def map_stage(d, todo):
        for ci, p in todo:
            q = st[d]["q"][ci, p]
            u = st[d]["u"][ci, p]
            vz = jnp.concatenate([zeros_pair, q["v"]], axis=1)
            on = lambda h, t: jnp.where(head2[h], t, 0.0)
            ry = _mm(st[d]["a4"][ci, p][c:], jnp.concatenate([on(0, u), on(0, vz), on(1, vz), on(1, u)], axis=0))
            pq = _mm_tn(jnp.concatenate([q["b_h"], q["k_h"]], axis=0), jnp.concatenate([u, vz], axis=0))
            rhat = q["r_t"] + ry[:, :PAIR_W]
            pmat = jnp.where(own_block, pq[:, :PAIR_W], 0.0) + jnp.where(diag, q["g_end"], 0.0)
            qmat = jnp.where(own_block, pq[:, PAIR_W:], 0.0)
            st[d]["maps"][ci, p] = (jnp.concatenate([rhat, pmat], axis=0).astype(BF16), ry[:, PAIR_W:], qmat)

    def carry_step(d, step):
        ci = step if d == 0 else nch - 1 - step
        for p in range(N_PAIRS):
            rows, lanes = sl(ci, p)
            lhs, yi, qmat = st[d]["maps"][ci, p]
            out = _mm(lhs, jnp.concatenate(_split(h_ref[d, p]), axis=1))
            out = out[:, :PAIR_W] + out[:, PAIR_W:]
            y_refs[d][rows, lanes] = out[:c] + yi
            h_ref[d, p] = out[c:] + qmat

    chunk = lambda ci: [(ci, p) for p in range(N_PAIRS)]
    for ci in range(nch):
        prep_stage(0, ci)
    for ci in range(nch):
        score_stage(0, chunk(ci))
        prep_stage(1, ci)
    for ci in range(nch):
        rhs_stage(0, chunk(ci))
        score_stage(1, chunk(ci))
    rhs_stage(1, cps)
    for j in range(SOLVE_LEVELS):
        for d in range(2):
            solve_level(d, j)
    for d in range(2):
        apply_stage(d)
    for step in range(nch):
        for d in range(2):
            map_stage(d, chunk(step if d == 0 else nch - 1 - step))
        if step:
            for d in range(2):
                carry_step(d, step - 1)
    for d in range(2):
        carry_step(d, nch - 1)


def _wkv(rkvk, lw, nalr, batch, l, tri, k_a):
    n = rkvk.shape[0]
    nb = n // batch // WKV_BLOCK
    fwd = lambda wdt, col=0: pl.BlockSpec((WKV_BLOCK, wdt), lambda b, j: (b * nb + j, col))
    bwd = lambda wdt, col=0: pl.BlockSpec((WKV_BLOCK, wdt), lambda b, j: (b * nb + nb - 1 - j, col))
    return pl.pallas_call(
        _wkv_body,
        grid=(batch, nb),
        in_specs=[fwd(RKVK_W), fwd(GROUP_W, 0), fwd(GROUP_W, 0), bwd(RKVK_W), bwd(GROUP_W, 1), bwd(GROUP_W, 1),
                  _full(tri.shape), _layer(k_a, l)],
        out_specs=[fwd(GROUP_W), bwd(GROUP_W)],
        out_shape=[jax.ShapeDtypeStruct((n, GROUP_W), F32)] * 2,
        scratch_shapes=[pltpu.VMEM((2, N_PAIRS, PAIR_W, PAIR_W), F32)],
        compiler_params=pltpu.CompilerParams(dimension_semantics=("parallel", "arbitrary"),
                                             vmem_limit_bytes=VMEM_LIMIT),
        name="wkv7",
    )(rkvk, lw, nalr, rkvk, lw, nalr, tri, k_a)


def _na_body(q_ref, k_ref, v_ref, bias_ref, o_ref, *, rows):
    j = pl.program_id(1)
    lane = lax.broadcasted_iota(jnp.int32, (1, GROUP_W), 1) // HEAD_DIM

    def window(ref, start, c0):
        return jnp.concatenate([ref[pl.ds(start + i * GRID_W + c0, NA_WIN), :] for i in range(NA_KH)], axis=0)

    work = []
    for rr in range(NA_ROWS):
        r = j * NA_ROWS + rr
        rs = jnp.clip(r - NA_KH // 2, 0, rows - NA_KH)
        start = pl.multiple_of(rs * GRID_W, GRID_W)
        q = q_ref[rr * GRID_W:(rr + 1) * GRID_W, :].astype(F32) * (HEAD_DIM ** -0.5)
        bias = bias_ref[rs - r + (NA_KH - 1)]
        off = 0
        for q0, nq, c0 in NA_BLOCKS:
            qb = q[q0:q0 + nq]
            qs = jnp.concatenate([jnp.where(lane == h, qb, 0.0) for h in range(HEADS)], axis=0).astype(BF16)
            s = _mm_nt(qs, window(k_ref, start, c0)) + bias[off:off + HEADS * nq]
            work.append((start, nq, c0, s))
            off += HEADS * nq
    soft = []
    for start, nq, c0, s in work:
        p = jnp.exp(s - jnp.max(s, axis=-1, keepdims=True))
        soft.append((p.astype(BF16), jnp.sum(p, axis=-1, keepdims=True)))
    outs = []
    for (start, nq, c0, _), (p, norm) in zip(work, soft):
        o = _mm(p, window(v_ref, start, c0)) / norm
        acc = o[0:nq]
        for h in range(1, HEADS):
            acc = jnp.where(lane == h, o[h * nq:(h + 1) * nq], acc)
        outs.append(acc)
    per_row = len(NA_BLOCKS)
    for rr in range(NA_ROWS):
        o_ref[rr * GRID_W:(rr + 1) * GRID_W, :] = jnp.concatenate(
            outs[rr * per_row:(rr + 1) * per_row], axis=0).astype(o_ref.dtype)


def _na(na, bias, batch, l):
    n = na.shape[0]
    t = n // batch
    rows = t // GRID_W
    nj = rows // NA_ROWS
    blk = NA_ROWS * GRID_W
    return pl.pallas_call(
        functools.partial(_na_body, rows=rows),
        grid=(batch, nj),
        in_specs=[pl.BlockSpec((blk, GROUP_W), lambda b, j: (b * nj + j, 0)),
                  pl.BlockSpec((t, GROUP_W), lambda b, j: (b, 1)),
                  pl.BlockSpec((t, GROUP_W), lambda b, j: (b, 2)),
                  _layer(bias, l)],
        out_specs=pl.BlockSpec((blk, GROUP_W), lambda b, j: (b * nj + j, 0)),
        out_shape=jax.ShapeDtypeStruct((n, GROUP_W), BF16),
        compiler_params=pltpu.CompilerParams(dimension_semantics=("parallel", "arbitrary"),
                                             vmem_limit_bytes=VMEM_LIMIT),
        name="nattn",
    )(na, na, na, bias)


def _na_bias_table(rpb, rows):
    nl = rpb.shape[0]
    kh = min(NA_KH, rows)
    c = np.arange(GRID_W)
    col_start = np.clip(c - NA_KW // 2, 0, GRID_W - NA_KW)
    col_mask = (c[None, :] >= col_start[:, None]) & (c[None, :] < col_start[:, None] + NA_KW)
    dx = np.clip(c[None, :] - c[:, None], -(NA_KW - 1), NA_KW - 1) + (NA_KW - 1)
    pick_dx = (dx[None] == np.arange(2 * NA_KW - 1)[:, None, None]).astype(np.float32)
    dy = np.arange(NA_KH)[:, None] + np.arange(kh)[None, :]
    pick_dy = (dy[..., None] == np.arange(2 * NA_KH - 1)).astype(np.float32)
    b = jnp.einsum("lhyx,diy,xqw->ldhqiw", rpb, pick_dy, pick_dx, precision=lax.Precision.HIGHEST)
    b = jnp.where(col_mask[None, None, None, :, None, :], b, NEG_BIG)
    blocks = [b[:, :, :, q0:q0 + nq, :, c0:c0 + NA_WIN].reshape(nl, NA_KH, HEADS * nq, kh * NA_WIN)
              for q0, nq, c0 in NA_BLOCKS]
    return jnp.concatenate(blocks, axis=2)


def _seg_sum(x, seg):
    hi, lo = _split(x)
    return _mm(hi, seg) + _mm(lo, seg)


def _seg_mean(x, seg):
    return _seg_sum(x, seg) * (1.0 / HEAD_DIM)


def _merge_body(x_ref, conv_ref, cprev_ref, cnext_ref, rkv_ref, lora_ref, yf_ref, yb_ref, na_ref, sgu_ref,
                seg_ref, convw_ref, gup_ref, rk_ref, lnw_ref, lnb_ref, sgn_ref, sgw_ref, sgb_ref,
                mg_ref, wout_ref, gpost_ref, o_ref, *, tiles_per_seq):
    tm = ROW_TILE
    i = pl.program_id(0)
    seg = seg_ref[...]

    conv = conv_ref[...].astype(F32)
    z = conv[:, 2 * GROUP_W:] * conv[:, :GROUP_W]
    zp = cprev_ref[15:16, :].astype(F32)
    zp = zp[:, 2 * GROUP_W:] * zp[:, :GROUP_W]
    zn = cnext_ref[0:1, :].astype(F32)
    zn = zn[:, 2 * GROUP_W:] * zn[:, :GROUP_W]
    first = (i % tiles_per_seq) == 0
    last = (i % tiles_per_seq) == tiles_per_seq - 1
    zp = jnp.where(first, 0.0, zp)
    zn = jnp.where(last, 0.0, zn)
    ridx = lax.broadcasted_iota(jnp.int32, (tm, 1), 0)
    z_prev = jnp.where(ridx == 0, zp, pltpu.roll(z, 1, axis=0))
    z_next = jnp.where(ridx == tm - 1, zn, pltpu.roll(z, tm - 1, axis=0))
    cw = convw_ref[...]
    y_conv = conv[:, GROUP_W:2 * GROUP_W] * (z_prev * cw[0:1] + z * cw[1:2] + z_next * cw[2:3])

    lane = lax.broadcasted_iota(jnp.int32, (1, GROUP_W), 1) // HEAD_DIM
    sgw = sgw_ref[...]

    def mixers(rows):
        rkv = rkv_ref[rows, :].astype(F32)
        r, k, v = rkv[:, :GROUP_W], rkv[:, GROUP_W:2 * GROUP_W], rkv[:, 2 * GROUP_W:]
        y = yf_ref[rows, :] + yb_ref[rows, :]
        mu = _seg_mean(y, seg)
        yc = y - mu
        var = _mm((yc * yc).astype(BF16), seg) * (1.0 / HEAD_DIM)
        yn = yc * lax.rsqrt(var + GN_EPS) * lnw_ref[...] + lnb_ref[...]
        bonus = _mm((r * k * rk_ref[...]).astype(BF16), seg) * v
        gate = _dot(_sigmoid(lora_ref[rows, :]), gup_ref[...])
        y_rwkv = (yn + bonus) * gate

        sg = sgu_ref[rows, :].astype(F32)
        u = jax.nn.gelu(sg[:, :GROUP_W])
        gv = jax.nn.gelu(sg[:, GROUP_W:])
        gmu = jnp.mean(gv, axis=-1, keepdims=True)
        gc = gv - gmu
        gvn = gc * lax.rsqrt(jnp.mean(gc * gc, axis=-1, keepdims=True) + NORM_EPS) * sgn_ref[...]
        mixed = []
        for ci in range((rows.stop - rows.start) // SGU_CHUNK):
            res = _dot(sgw, gvn[ci * SGU_CHUNK:(ci + 1) * SGU_CHUNK])
            m = res[0:SGU_CHUNK]
            for h in range(1, HEADS):
                m = jnp.where(lane == h, res[h * SGU_CHUNK:(h + 1) * SGU_CHUNK], m)
            mixed.append(m + sgb_ref[...])
        y_sgu = u * jnp.concatenate(mixed, axis=0)

        y_na = na_ref[rows, :].astype(F32)
        merged = jnp.concatenate([_rms(y_conv[rows]), _rms(y_rwkv), _rms(y_na), _rms(y_sgu)], axis=1)
        return (merged * mg_ref[...]).astype(BF16)

    half = tm // 2
    for r0 in (0, half):
        rows = slice(r0, r0 + half)
        out = jnp.dot(mixers(rows), wout_ref[...], preferred_element_type=F32)
        o_ref[rows, :] = x_ref[rows, :] + _rms(out) * gpost_ref[...]


def _merge(x2, conv, rkv, lora, yf, yb, y_na, sgu, batch, l, seg, convw, gup, rk, lnw, lnb, sgn, sgw, sgb,
           mg, wout, gpost):
    n = x2.shape[0]
    tm = ROW_TILE
    tiles_per_seq = n // batch // tm
    nh = n // 16
    row = lambda wdt: pl.BlockSpec((tm, wdt), lambda i: (i, 0))
    prev = pl.BlockSpec((16, 3 * GROUP_W), lambda i: (jnp.maximum(i * (tm // 16) - 1, 0), 0))
    nxt = pl.BlockSpec((16, 3 * GROUP_W), lambda i: (jnp.minimum((i + 1) * (tm // 16), nh - 1), 0))
    params = (seg, convw, gup, rk, lnw, lnb, sgn, sgw, sgb, mg, wout, gpost)
    return pl.pallas_call(
        functools.partial(_merge_body, tiles_per_seq=tiles_per_seq),
        grid=(n // tm,),
        in_specs=[row(D_MODEL), row(3 * GROUP_W), prev, nxt, row(3 * GROUP_W), row(LORA_W),
                  row(GROUP_W), row(GROUP_W), row(GROUP_W), row(2 * GROUP_W)]
                 + [_full(p.shape) if p is seg else _layer(p, l) for p in params],
        out_specs=row(D_MODEL),
        out_shape=jax.ShapeDtypeStruct((n, D_MODEL), F32),
        compiler_params=pltpu.CompilerParams(dimension_semantics=("parallel",),
                                             vmem_limit_bytes=VMEM_LIMIT),
        name="merge",
    )(x2, conv, conv, conv, rkv, lora, yf, yb, y_na, sgu, *params)


FFN_SLABS = ((0, 768), (768, 768), (1536, 768), (2304, 512))
assert sum(w for _, w in FFN_SLABS) == D_FF
HALO = 8


def _ffn_body(x_ref, xprev_ref, xnext_ref, gpre_ref, wup_ref, cw_ref, wdn_ref, gpost_ref, o_ref, *,
              tiles_per_seq):
    tm = ROW_TILE
    i = pl.program_id(0)
    first = (i % tiles_per_seq) == 0
    last = (i % tiles_per_seq) == tiles_per_seq - 1
    x = x_ref[...]
    g = gpre_ref[...]
    hf = _rms(x) * g
    hp = jnp.where(first, 0.0, _rms(xprev_ref[...]) * g)
    hn = jnp.where(last, 0.0, _rms(xnext_ref[...]) * g)
    h = hf.astype(BF16)
    hext = jnp.concatenate([hp, hf, hn], axis=0).astype(BF16)
    ext = tm + 2 * HALO

    def up(s):
        c0, w = FFN_SLABS[s]
        gate = jnp.dot(hext, wup_ref[:, c0:c0 + w], preferred_element_type=F32)
        lin = jnp.dot(h, wup_ref[:, D_FF + c0:D_FF + c0 + w], preferred_element_type=F32)
        return gate, lin

    def down(s, gate, lin):
        c0, w = FFN_SLABS[s]
        cw = cw_ref[:, c0:c0 + w]
        g_prev = pltpu.roll(gate, 1, axis=0)[HALO:HALO + tm]
        g_next = pltpu.roll(gate, ext - 1, axis=0)[HALO:HALO + tm]
        cv = g_prev * cw[0:1] + gate[HALO:HALO + tm] * cw[1:2] + g_next * cw[2:3]
        hid = jax.nn.gelu(cv) * lin
        return jnp.dot(hid.astype(BF16), wdn_ref[c0:c0 + w, :], preferred_element_type=F32)

    acc = None
    pending = up(0)
    for s in range(len(FFN_SLABS)):
        nxt = up(s + 1) if s + 1 < len(FFN_SLABS) else None
        part = down(s, *pending)
        acc = part if acc is None else acc + part
        pending = nxt
    o_ref[...] = x + _rms(acc) * gpost_ref[...]


def _ffn(x2, batch, l, gpre, wup, cw, wdn, gpost):
    n = x2.shape[0]
    tm = ROW_TILE
    tiles_per_seq = n // batch // tm
    nh = n // HALO
    row = pl.BlockSpec((tm, D_MODEL), lambda i: (i, 0))
    prev = pl.BlockSpec((HALO, D_MODEL), lambda i: (jnp.maximum(i * (tm // HALO) - 1, 0), 0))
    nxt = pl.BlockSpec((HALO, D_MODEL), lambda i: (jnp.minimum((i + 1) * (tm // HALO), nh - 1), 0))
    params = (gpre, wup, cw, wdn, gpost)
    return pl.pallas_call(
        functools.partial(_ffn_body, tiles_per_seq=tiles_per_seq),
        grid=(n // tm,),
        in_specs=[row, prev, nxt] + [_layer(p, l) for p in params],
        out_specs=row,
        out_shape=jax.ShapeDtypeStruct((n, D_MODEL), F32),
        compiler_params=pltpu.CompilerParams(dimension_semantics=("parallel",),
                                             vmem_limit_bytes=VMEM_LIMIT),
        name="convffn",
    )(x2, x2, x2, *params)


def _regroup_w_in(w):
    g = GROUP_W
    lora0 = 6 * g
    lora1 = lora0 + 2 * DECAY_RANK + 2 * ICLR_RANK + GATE_RANK
    w = w.astype(BF16)
    pad = jnp.zeros(w.shape[:-1] + (LORA_W - (lora1 - lora0),), BF16)
    return jnp.concatenate([w[..., :lora0], w[..., lora1:lora1 + 5 * g], w[..., lora0:lora1], pad], axis=-1)


def _lora_up(w, start):
    return jnp.pad(w, ((0, 0), (start, LORA_W - start - w.shape[1]), (0, 0)))


def _all_params(p, rows):
    nl = p["w_in"].shape[0]
    both = lambda w, start: jnp.concatenate(
        [_lora_up(w[:, d], start + d * w.shape[2]) for d in range(2)], axis=-1)
    wd = both(p["rwkv_w_up"], 0)[:, :DECAY_LANES]
    wd_hi = wd.astype(BF16)
    wd_lo = (wd - wd_hi.astype(F32)).astype(BF16)
    wup = jnp.concatenate([wd_hi, wd_lo, wd_hi, jnp.zeros_like(wd_hi)], axis=1)
    row2 = lambda a: a.reshape(nl, 1, -1)
    return dict(
        w_in=_regroup_w_in(p["w_in"]), g_mix_pre=row2(p["norm_mix_pre"]),
        w0=row2(p["rwkv_w0"]), wup=wup, a0=row2(p["rwkv_a0"]),
        aup=both(p["rwkv_a_up"], 2 * DECAY_RANK).astype(BF16),
        k_k=row2(p["rwkv_k_k"]), k_a=row2(p["rwkv_k_a"]),
        bias=_na_bias_table(p["na_rpb"], rows),
        convw=p["conv_a_w"], gup=_lora_up(p["rwkv_g_up"], 2 * DECAY_RANK + 2 * ICLR_RANK).astype(BF16),
        rk=row2(p["rwkv_r_k"]), lnw=row2(p["rwkv_lnx_w"]), lnb=row2(p["rwkv_lnx_b"]), sgn=row2(p["sgu_norm"]),
        sgw=p["sgu_w"].reshape(nl, HEADS * SGU_CHUNK, SGU_CHUNK).astype(BF16),
        sgb=jnp.repeat(jnp.swapaxes(p["sgu_b"], 1, 2), HEAD_DIM, axis=2),
        mg=row2(p["merge_gain"]), wout=p["w_out"].astype(BF16), g_mix_post=row2(p["norm_mix_post"]),
        g_ffn_pre=row2(p["norm_ffn_pre"]), ffn_up=p["ffn_w_up"].astype(BF16), ffn_conv=p["ffn_conv"],
        ffn_down=p["ffn_w_down"].astype(BF16), g_ffn_post=row2(p["norm_ffn_post"]),
    )


def _constants():
    i = jnp.arange(WKV_CHUNK)
    tri = jnp.stack([i[None, :] <= i[:, None], i[None, :] >= i[:, None]]).astype(BF16)
    g = jnp.arange(GROUP_W) // HEAD_DIM
    seg = (g[:, None] == g[None, :]).astype(BF16)
    return tri, seg


def _forward(x, p):
    batch, t, _ = x.shape
    assert t % ROW_TILE == 0 and t % (NA_ROWS * GRID_W) == 0 and t // GRID_W >= NA_KH
    assert (batch * t) % INPROJ_TILE == 0 and t % WKV_BLOCK == 0
    x2 = x.reshape(batch * t, D_MODEL)
    tri, seg = _constants()
    q = _all_params(p, t // GRID_W)
    for l in range(p["w_in"].shape[0]):
        conv, rkvk, na, sgu, lora, lw, nalr = _inproj(x2, l, q["g_mix_pre"], q["w_in"], seg, q["k_k"], q["w0"],
                                                      q["wup"], q["a0"], q["aup"])
        yf, yb = _wkv(rkvk, lw, nalr, batch, l, tri, q["k_a"])
        y_na = _na(na, q["bias"], batch, l)
        x2 = _merge(x2, conv, rkvk, lora, yf, yb, y_na, sgu, batch, l, seg, q["convw"], q["gup"], q["rk"],
                    q["lnw"], q["lnb"], q["sgn"], q["sgw"], q["sgb"], q["mg"], q["wout"], q["g_mix_post"])
        x2 = _ffn(x2, batch, l, q["g_ffn_pre"], q["ffn_up"], q["ffn_conv"], q["ffn_down"], q["g_ffn_post"])
    return x2.reshape(batch, t, D_MODEL)


def kernel(x, norm_mix_pre, norm_mix_post, norm_ffn_pre, norm_ffn_post, w_in, conv_a_w, rwkv_w0, rwkv_w_up,
           rwkv_a0, rwkv_a_up, rwkv_g_up, rwkv_k_k, rwkv_k_a, rwkv_r_k, rwkv_lnx_w, rwkv_lnx_b, na_rpb,
           sgu_norm, sgu_w, sgu_b, merge_gain, w_out, ffn_w_up, ffn_conv, ffn_w_down):
    p = dict(norm_mix_pre=norm_mix_pre, norm_mix_post=norm_mix_post, norm_ffn_pre=norm_ffn_pre,
             norm_ffn_post=norm_ffn_post, w_in=w_in, conv_a_w=conv_a_w, rwkv_w0=rwkv_w0, rwkv_w_up=rwkv_w_up,
             rwkv_a0=rwkv_a0, rwkv_a_up=rwkv_a_up, rwkv_g_up=rwkv_g_up, rwkv_k_k=rwkv_k_k, rwkv_k_a=rwkv_k_a,
             rwkv_r_k=rwkv_r_k, rwkv_lnx_w=rwkv_lnx_w, rwkv_lnx_b=rwkv_lnx_b, na_rpb=na_rpb, sgu_norm=sgu_norm,
             sgu_w=sgu_w, sgu_b=sgu_b, merge_gain=merge_gain, w_out=w_out, ffn_w_up=ffn_w_up, ffn_conv=ffn_conv,
             ffn_w_down=ffn_w_down)
    return _forward(x, p)
```

```python
import functools
import math

import jax
import jax.numpy as jnp
import numpy as np
from jax import lax
from jax.experimental import pallas as pl
from jax.experimental.pallas import tpu as pltpu

F32 = jnp.float32
BF16 = jnp.bfloat16

D_MODEL = 1024
GRID_W = 64
HEADS = 4
HEAD_DIM = 64
GROUP_W = HEADS * HEAD_DIM
DECAY_RANK = 16
ICLR_RANK = 16
GATE_RANK = 32
LORA_W = 128
DECAY_SCALE = math.exp(-0.5)
GN_EPS = 64e-5
NA_KH = 8
NA_KW = 16
SGU_CHUNK = 128
D_FF = 2816
NORM_EPS = 1e-6
NEG_BIG = -1e30

WKV_CHUNK = 64
WKV_BLOCK = 512
ROW_TILE = 512
NA_ROWS = 16
NA_WIN = 2 * NA_KW
NA_ALIGN = 16


def _na_blocks():
    start_of = lambda q: min(max(q - NA_KW // 2, 0), GRID_W - NA_KW)
    blocks, q0 = [], 0
    while q0 < GRID_W:
        c0 = min(start_of(q0) // NA_ALIGN * NA_ALIGN, GRID_W - NA_WIN)
        q1 = q0
        while q1 < GRID_W and c0 <= start_of(q1) and start_of(q1) + NA_KW <= c0 + NA_WIN:
            q1 += 1
        nq = (q1 - q0) // 8 * 8
        blocks.append((q0, nq, c0))
        q0 += nq
    return tuple(blocks)


NA_BLOCKS = _na_blocks()
assert all(q0 % 8 == 0 and nq % 8 == 0 for q0, nq, _ in NA_BLOCKS)
VMEM_LIMIT = 56 * 1024 * 1024


def _dot(a, b):
    return jnp.dot(a.astype(BF16), b.astype(BF16), preferred_element_type=F32)


def _split(x):
    hi = x.astype(BF16)
    lo = (x - hi.astype(F32)).astype(BF16)
    return hi, lo


def _dot_x3(a, b):
    ah, al = _split(a)
    bh, bl = _split(b)
    d = lambda p, q: jnp.dot(p, q, preferred_element_type=F32)
    return d(ah, bh) + (d(ah, bl) + d(al, bh))


def _sigmoid(x):
    return 1.0 / (1.0 + jnp.exp(-x))


def _rms(x, eps=NORM_EPS):
    return x * lax.rsqrt(jnp.mean(x * x, axis=-1, keepdims=True) + eps)


def _full(shape):
    n = len(shape)
    return pl.BlockSpec(shape, lambda *_: (0,) * n)


def _layer(a, l):
    n = a.ndim - 1
    return pl.BlockSpec((None,) + a.shape[1:], lambda *_: (l,) + (0,) * n)


IN_GROUPS = (3 * GROUP_W, 3 * GROUP_W, 3 * GROUP_W, 2 * GROUP_W, LORA_W)
IN_COLS = sum(IN_GROUPS)
RKVK_W = 4 * GROUP_W


INPROJ_TILE = 1024
INPROJ_PIECE = 256
DECAY_LANES = 2 * DECAY_RANK


def _inproj_body(x_ref, g_ref, w_ref, seg_ref, kk_ref, w0_ref, wd_ref, a0_ref, wa_ref,
                 conv_ref, rkvk_ref, na_ref, sgu_ref, lora_ref, lw_ref, nalr_ref):
    seg = seg_ref[...]
    lane = lax.broadcasted_iota(jnp.int32, (1, LORA_W), 1)

    def project(rows):
        h = _rms(x_ref[rows, :]) * g_ref[...]
        p = jnp.dot(h.astype(BF16), w_ref[...], preferred_element_type=F32)
        o = 0
        for ref, wdt in zip((conv_ref, rkvk_ref, na_ref, sgu_ref, lora_ref), IN_GROUPS):
            ref[rows, 0:wdt] = p[:, o:o + wdt].astype(ref.dtype)
            o += wdt
        return rows, p[:, IN_GROUPS[0] + GROUP_W:IN_GROUPS[0] + 2 * GROUP_W], p[:, IN_COLS - LORA_W:]

    def tokenwise(rows, k, lora):
        kk = k * kk_ref[...]
        rkvk_ref[rows, 3 * GROUP_W:] = (kk * lax.rsqrt(_mm((kk * kk).astype(BF16), seg) + 1e-12)).astype(BF16)
        th = jnp.tanh(lora)
        hi = th.astype(BF16).astype(F32)
        packed = jnp.where(lane < DECAY_LANES, hi,
                           jnp.where(lane < 2 * DECAY_LANES, pltpu.roll(hi, DECAY_LANES, axis=1),
                                     jnp.where(lane < 3 * DECAY_LANES, pltpu.roll(th - hi, 2 * DECAY_LANES, axis=1),
                                               0.0)))
        lw_ref[rows, :] = -DECAY_SCALE * _sigmoid(w0_ref[...] + _mm(packed.astype(BF16), wd_ref[...]))
        nalr_ref[rows, :] = (-_sigmoid(a0_ref[...] + _mm(lora.astype(BF16), wa_ref[...]))).astype(BF16)

    pending = None
    for r0 in range(0, INPROJ_TILE, INPROJ_PIECE):
        done = project(slice(r0, r0 + INPROJ_PIECE))
        if pending is not None:
            tokenwise(*pending)
        pending = done
    tokenwise(*pending)


def _inproj(x2, l, gain, w, seg, k_k, w0, wup, a0, aup):
    n = x2.shape[0]
    tm = INPROJ_TILE
    row = lambda wdt: pl.BlockSpec((tm, wdt), lambda i: (i, 0))
    params = (gain, w, seg, k_k, w0, wup, a0, aup)
    specs = [_full(q.shape) if q is seg else _layer(q, l) for q in params]
    widths = (IN_GROUPS[0], RKVK_W, IN_GROUPS[2], IN_GROUPS[3], IN_GROUPS[4], 2 * GROUP_W, 2 * GROUP_W)
    dtypes = (BF16, BF16, BF16, BF16, F32, F32, BF16)
    return pl.pallas_call(
        _inproj_body,
        grid=(n // tm,),
        in_specs=[row(D_MODEL)] + specs,
        out_specs=[row(w_) for w_ in widths],
        out_shape=[jax.ShapeDtypeStruct((n, w_), dt) for w_, dt in zip(widths, dtypes)],
        compiler_params=pltpu.CompilerParams(dimension_semantics=("parallel",),
                                             vmem_limit_bytes=VMEM_LIMIT),
        name="inproj",
    )(x2, *params)


PAIR_W = 2 * HEAD_DIM
N_PAIRS = HEADS // 2
SOLVE_LEVELS = 6


def _mm(a, b):
    return jnp.dot(a, b, preferred_element_type=F32)


def _mm_nt(a, b):
    return lax.dot_general(a, b, (((1,), (1,)), ((), ())), preferred_element_type=F32)


def _mm_tn(a, b):
    return lax.dot_general(a, b, (((0,), (0,)), ((), ())), preferred_element_type=F32)


def _mm_x3(a, b):
    return _mm(a[0], b[0]) + (_mm(a[0], b[1]) + _mm(a[1], b[0]))


def _wkv_prep(rkvk_ref, lw_ref, nalr_ref, tri, ka_gain, d, ci):
    c = WKV_CHUNK
    pieces = {}
    rows = slice(ci * c, (ci + 1) * c)
    lw = lw_ref[rows, :]
    lw_hi, lw_lo = _split(lw)
    cum = _mm(tri, lw_hi) + _mm(tri, lw_lo)
    g_end = jnp.exp(cum[c - 1:c] if d == 0 else cum[0:1])
    for p in range(N_PAIRS):
        lanes = slice(p * PAIR_W, (p + 1) * PAIR_W)
        part = lambda g: rkvk_ref[rows, g * GROUP_W + p * PAIR_W:g * GROUP_W + (p + 1) * PAIR_W]
        r, k, v, kk = part(0).astype(F32), part(1).astype(F32), part(2), part(3).astype(F32)
        nalr = nalr_ref[rows, lanes].astype(F32)
        k_eff = k * (1.0 - (1.0 + nalr) * ka_gain[:, lanes])
        b_vec = kk * nalr
        g_inc = jnp.exp(cum[:, lanes])
        g_inv = 1.0 / g_inc
        g_hat = g_end[:, lanes] * g_inv
        pieces[ci, p] = dict(
            a_t=kk * jnp.exp(cum[:, lanes] - lw[:, lanes]), r_t=r * g_inc,
            b_t=(b_vec * g_inv).astype(BF16), k_t=(k_eff * g_inv).astype(BF16),
            b_h=(b_vec * g_hat).astype(BF16), k_h=(k_eff * g_hat).astype(BF16),
            v=v, g_end=g_end[:, lanes])
    return pieces


def _wkv_body(rkvk_f, lw_f, nalr_f, rkvk_b, lw_b, nalr_b, tri_ref, ka_ref, yf_ref, yb_ref, h_ref):
    @pl.when(pl.program_id(1) == 0)
    def _():
        h_ref[...] = jnp.zeros_like(h_ref)

    c = WKV_CHUNK
    nch = WKV_BLOCK // c
    in_refs = ((rkvk_f, lw_f, nalr_f), (rkvk_b, lw_b, nalr_b))
    y_refs = (yf_ref, yb_ref)

    def head_lanes(pairs_wide):
        lane = lax.broadcasted_iota(jnp.int32, (1, pairs_wide * PAIR_W), 1)
        return [(lane % PAIR_W) // HEAD_DIM == h for h in range(2)]

    head1, head2 = head_lanes(1), head_lanes(2)
    row2 = lax.broadcasted_iota(jnp.int32, (PAIR_W, 2 * PAIR_W), 0)
    col2 = lax.broadcasted_iota(jnp.int32, (PAIR_W, 2 * PAIR_W), 1)
    t_idx, s_idx = row2 % c, col2 % c
    tri_mask = [(s_idx < t_idx) | ((row2 >= c) & (s_idx == t_idx)),
                (s_idx > t_idx) | ((row2 >= c) & (s_idx == t_idx))]
    row = lax.broadcasted_iota(jnp.int32, (PAIR_W, PAIR_W), 0)
    col = lax.broadcasted_iota(jnp.int32, (PAIR_W, PAIR_W), 1)
    diag = row == col
    own_block = (row // c) == (col // c)
    zeros_pair = jnp.zeros((c, PAIR_W), BF16)

    cps = [(ci, p) for ci in range(nch) for p in range(N_PAIRS)]
    sl = lambda ci, p: (slice(ci * c, (ci + 1) * c), slice(p * PAIR_W, (p + 1) * PAIR_W))
    st = [dict(q={}, a4={}, x={}, t={}, u={}, s_bf={}, maps={}) for _ in range(2)]
    eye_pair = jnp.where(lax.broadcasted_iota(jnp.int32, (c, PAIR_W), 1) % c
                         == lax.broadcasted_iota(jnp.int32, (c, PAIR_W), 0), 1.0, 0.0)

    def prep_stage(d, ci):
        rkvk_ref, lw_ref, nalr_ref = in_refs[d]
        st[d]["q"].update(_wkv_prep(rkvk_ref, lw_ref, nalr_ref, tri_ref[d], ka_ref[...], d, ci))

    def score_stage(d, todo):
        for ci, p in todo:
            q = st[d]["q"][ci, p]
            lhs = jnp.concatenate([q["a_t"], q["r_t"]], axis=0).astype(BF16)
            b_t, k_t = q["b_t"], q["k_t"]
            rhs = jnp.concatenate([jnp.where(head1[0], jnp.concatenate([b_t, k_t], axis=0), 0.0),
                                   jnp.where(head1[1], jnp.concatenate([k_t, b_t], axis=0), 0.0)], axis=0)
            st[d]["a4"][ci, p] = jnp.where(tri_mask[d], _mm_nt(lhs, rhs), 0.0).astype(BF16)

    def rhs_stage(d, todo):
        for ci, p in todo:
            q = st[d]["q"][ci, p]
            top = st[d]["a4"][ci, p][:c]
            v = [jnp.where(head1[h], q["v"], 0.0) for h in range(2)]
            akv = _mm(top, jnp.concatenate([zeros_pair, v[0], v[1], zeros_pair], axis=0))
            st[d]["x"][ci, p] = jnp.concatenate([q["a_t"].astype(BF16), akv.astype(BF16)], axis=1)
            st[d]["s_bf"][ci, p] = jnp.where(head1[0], top[:, :PAIR_W], top[:, PAIR_W:])
            st[d]["t"][ci, p] = eye_pair

    def by_head(cat, heads):
        return jnp.concatenate([jnp.where(heads[0], cat, 0.0), jnp.where(heads[1], cat, 0.0)], axis=0)

    def solve_level(d, j):
        s_bf, t = st[d]["s_bf"], st[d]["t"]
        last = j == SOLVE_LEVELS - 1
        for key in cps:
            t_bf = t[key].astype(BF16)
            if last:
                t[key] = t[key] + _mm(s_bf[key], by_head(t_bf, head1))
            else:
                out = _mm(s_bf[key], by_head(jnp.concatenate([s_bf[key], t_bf], axis=1), head2))
                s_bf[key] = out[:, :PAIR_W].astype(BF16)
                t[key] = t[key] + out[:, PAIR_W:]

    def apply_stage(d):
        for key in cps:
            st[d]["u"][key] = _mm(st[d]["t"][key].astype(BF16), by_head(st[d]["x"][key], head2)).astype(BF16)

    def map_stage(d, todo):
        for ci, p in todo:
            q = st[d]["q"][ci, p]
            u = st[d]["u"][ci, p]
            vz = jnp.concatenate([zeros_pair, q["v"]], axis=1)
            on = lambda h, t: jnp.where(head2[h], t, 0.0)
            ry = _mm(st[d]["a4"][ci, p][c:], jnp.concatenate([on(0, u), on(0, vz), on(1, vz), on(1, u)], axis=0))
            pq = _mm_tn(jnp.concatenate([q["b_h"], q["k_h"]], axis=0), jnp.concatenate([u, vz], axis=0))
            rhat = q["r_t"] + ry[:, :PAIR_W]
            pmat = jnp.where(own_block, pq[:, :PAIR_W], 0.0) + jnp.where(diag, q["g_end"], 0.0)
            qmat = jnp.where(own_block, pq[:, PAIR_W:], 0.0)
            st[d]["maps"][ci, p] = (jnp.concatenate([rhat, pmat], axis=0).astype(BF16), ry[:, PAIR_W:], qmat)

    def carry_step(d, step):
        ci = step if d == 0 else nch - 1 - step
        for p in range(N_PAIRS):
            rows, lanes = sl(ci, p)
            lhs, yi, qmat = st[d]["maps"][ci, p]
            out = _mm(lhs, jnp.concatenate(_split(h_ref[d, p]), axis=1))
            out = out[:, :PAIR_W] + out[:, PAIR_W:]
            y_refs[d][rows, lanes] = out[:c] + yi
            h_ref[d, p] = out[c:] + qmat

    chunk = lambda ci: [(ci, p) for p in range(N_PAIRS)]
    for ci in range(nch):
        prep_stage(0, ci)
    for ci in range(nch):
        score_stage(0, chunk(ci))
        prep_stage(1, ci)
    for ci in range(nch):
        rhs_stage(0, chunk(ci))
        score_stage(1, chunk(ci))
    rhs_stage(1, cps)
    for j in range(SOLVE_LEVELS):
        for d in range(2):
            solve_level(d, j)
    for d in range(2):
        apply_stage(d)
    for step in range(nch):
        for d in range(2):
            map_stage(d, chunk(step if d == 0 else nch - 1 - step))
        if step:
            for d in range(2):
                carry_step(d, step - 1)
    for d in range(2):
        carry_step(d, nch - 1)


def _wkv(rkvk, lw, nalr, batch, l, tri, k_a):
    n = rkvk.shape[0]
    nb = n // batch // WKV_BLOCK
    fwd = lambda wdt, col=0: pl.BlockSpec((WKV_BLOCK, wdt), lambda b, j: (b * nb + j, col))
    bwd = lambda wdt, col=0: pl.BlockSpec((WKV_BLOCK, wdt), lambda b, j: (b * nb + nb - 1 - j, col))
    return pl.pallas_call(
        _wkv_body,
        grid=(batch, nb),
        in_specs=[fwd(RKVK_W), fwd(GROUP_W, 0), fwd(GROUP_W, 0), bwd(RKVK_W), bwd(GROUP_W, 1), bwd(GROUP_W, 1),
                  _full(tri.shape), _layer(k_a, l)],
        out_specs=[fwd(GROUP_W), bwd(GROUP_W)],
        out_shape=[jax.ShapeDtypeStruct((n, GROUP_W), F32)] * 2,
        scratch_shapes=[pltpu.VMEM((2, N_PAIRS, PAIR_W, PAIR_W), F32)],
        compiler_params=pltpu.CompilerParams(dimension_semantics=("parallel", "arbitrary"),
                                             vmem_limit_bytes=VMEM_LIMIT),
        name="wkv7",
    )(rkvk, lw, nalr, rkvk, lw, nalr, tri, k_a)


def _na_body(q_ref, k_ref, v_ref, bias_ref, o_ref, *, rows):
    j = pl.program_id(1)
    lane = lax.broadcasted_iota(jnp.int32, (1, GROUP_W), 1) // HEAD_DIM

    def window(ref, start, c0):
        return jnp.concatenate([ref[pl.ds(start + i * GRID_W + c0, NA_WIN), :] for i in range(NA_KH)], axis=0)

    work = []
    for rr in range(NA_ROWS):
        r = j * NA_ROWS + rr
        rs = jnp.clip(r - NA_KH // 2, 0, rows - NA_KH)
        start = pl.multiple_of(rs * GRID_W, GRID_W)
        q = q_ref[rr * GRID_W:(rr + 1) * GRID_W, :].astype(F32) * (HEAD_DIM ** -0.5)
        bias = bias_ref[rs - r + (NA_KH - 1)]
        off = 0
        for q0, nq, c0 in NA_BLOCKS:
            qb = q[q0:q0 + nq]
            qs = jnp.concatenate([jnp.where(lane == h, qb, 0.0) for h in range(HEADS)], axis=0).astype(BF16)
            s = _mm_nt(qs, window(k_ref, start, c0)) + bias[off:off + HEADS * nq]
            work.append((start, nq, c0, s))
            off += HEADS * nq
    soft = []
    for start, nq, c0, s in work:
        p = jnp.exp(s - jnp.max(s, axis=-1, keepdims=True))
        soft.append((p.astype(BF16), jnp.sum(p, axis=-1, keepdims=True)))
    outs = []
    for (start, nq, c0, _), (p, norm) in zip(work, soft):
        o = _mm(p, window(v_ref, start, c0)) / norm
        acc = o[0:nq]
        for h in range(1, HEADS):
            acc = jnp.where(lane == h, o[h * nq:(h + 1) * nq], acc)
        outs.append(acc)
    per_row = len(NA_BLOCKS)
    for rr in range(NA_ROWS):
        o_ref[rr * GRID_W:(rr + 1) * GRID_W, :] = jnp.concatenate(
            outs[rr * per_row:(rr + 1) * per_row], axis=0).astype(o_ref.dtype)


def _na(na, bias, batch, l):
    n = na.shape[0]
    t = n // batch
    rows = t // GRID_W
    nj = rows // NA_ROWS
    blk = NA_ROWS * GRID_W
    return pl.pallas_call(
        functools.partial(_na_body, rows=rows),
        grid=(batch, nj),
        in_specs=[pl.BlockSpec((blk, GROUP_W), lambda b, j: (b * nj + j, 0)),
                  pl.BlockSpec((t, GROUP_W), lambda b, j: (b, 1)),
                  pl.BlockSpec((t, GROUP_W), lambda b, j: (b, 2)),
                  _layer(bias, l)],
        out_specs=pl.BlockSpec((blk, GROUP_W), lambda b, j: (b * nj + j, 0)),
        out_shape=jax.ShapeDtypeStruct((n, GROUP_W), BF16),
        compiler_params=pltpu.CompilerParams(dimension_semantics=("parallel", "arbitrary"),
                                             vmem_limit_bytes=VMEM_LIMIT),
        name="nattn",
    )(na, na, na, bias)


def _na_bias_table(rpb, rows):
    nl = rpb.shape[0]
    kh = min(NA_KH, rows)
    c = np.arange(GRID_W)
    col_start = np.clip(c - NA_KW // 2, 0, GRID_W - NA_KW)
    col_mask = (c[None, :] >= col_start[:, None]) & (c[None, :] < col_start[:, None] + NA_KW)
    dx = np.clip(c[None, :] - c[:, None], -(NA_KW - 1), NA_KW - 1) + (NA_KW - 1)
    pick_dx = (dx[None] == np.arange(2 * NA_KW - 1)[:, None, None]).astype(np.float32)
    dy = np.arange(NA_KH)[:, None] + np.arange(kh)[None, :]
    pick_dy = (dy[..., None] == np.arange(2 * NA_KH - 1)).astype(np.float32)
    b = jnp.einsum("lhyx,diy,xqw->ldhqiw", rpb, pick_dy, pick_dx, precision=lax.Precision.HIGHEST)
    b = jnp.where(col_mask[None, None, None, :, None, :], b, NEG_BIG)
    blocks = [b[:, :, :, q0:q0 + nq, :, c0:c0 + NA_WIN].reshape(nl, NA_KH, HEADS * nq, kh * NA_WIN)
              for q0, nq, c0 in NA_BLOCKS]
    return jnp.concatenate(blocks, axis=2)


def _seg_sum(x, seg):
    hi, lo = _split(x)
    return _mm(hi, seg) + _mm(lo, seg)


def _seg_mean(x, seg):
    return _seg_sum(x, seg) * (1.0 / HEAD_DIM)


def _merge_body(x_ref, conv_ref, cprev_ref, cnext_ref, rkv_ref, lora_ref, yf_ref, yb_ref, na_ref, sgu_ref,
                seg_ref, convw_ref, gup_ref, rk_ref, lnw_ref, lnb_ref, sgn_ref, sgw_ref, sgb_ref,
                mg_ref, wout_ref, gpost_ref, o_ref, *, tiles_per_seq):
    tm = ROW_TILE
    i = pl.program_id(0)
    seg = seg_ref[...]

    conv = conv_ref[...].astype(F32)
    z = conv[:, 2 * GROUP_W:] * conv[:, :GROUP_W]
    zp = cprev_ref[15:16, :].astype(F32)
    zp = zp[:, 2 * GROUP_W:] * zp[:, :GROUP_W]
    zn = cnext_ref[0:1, :].astype(F32)
    zn = zn[:, 2 * GROUP_W:] * zn[:, :GROUP_W]
    first = (i % tiles_per_seq) == 0
    last = (i % tiles_per_seq) == tiles_per_seq - 1
    zp = jnp.where(first, 0.0, zp)
    zn = jnp.where(last, 0.0, zn)
    ridx = lax.broadcasted_iota(jnp.int32, (tm, 1), 0)
    z_prev = jnp.where(ridx == 0, zp, pltpu.roll(z, 1, axis=0))
    z_next = jnp.where(ridx == tm - 1, zn, pltpu.roll(z, tm - 1, axis=0))
    cw = convw_ref[...]
    y_conv = conv[:, GROUP_W:2 * GROUP_W] * (z_prev * cw[0:1] + z * cw[1:2] + z_next * cw[2:3])

    lane = lax.broadcasted_iota(jnp.int32, (1, GROUP_W), 1) // HEAD_DIM
    sgw = sgw_ref[...]

    def mixers(rows):
        rkv = rkv_ref[rows, :].astype(F32)
        r, k, v = rkv[:, :GROUP_W], rkv[:, GROUP_W:2 * GROUP_W], rkv[:, 2 * GROUP_W:]
        y = yf_ref[rows, :] + yb_ref[rows, :]
        mu = _seg_mean(y, seg)
        yc = y - mu
        var = _mm((yc * yc).astype(BF16), seg) * (1.0 / HEAD_DIM)
        yn = yc * lax.rsqrt(var + GN_EPS) * lnw_ref[...] + lnb_ref[...]
        bonus = _mm((r * k * rk_ref[...]).astype(BF16), seg) * v
        gate = _dot(_sigmoid(lora_ref[rows, :]), gup_ref[...])
        y_rwkv = (yn + bonus) * gate

        sg = sgu_ref[rows, :].astype(F32)
        u = jax.nn.gelu(sg[:, :GROUP_W])
        gv = jax.nn.gelu(sg[:, GROUP_W:])
        gmu = jnp.mean(gv, axis=-1, keepdims=True)
        gc = gv - gmu
        gvn = gc * lax.rsqrt(jnp.mean(gc * gc, axis=-1, keepdims=True) + NORM_EPS) * sgn_ref[...]
        mixed = []
        for ci in range((rows.stop - rows.start) // SGU_CHUNK):
            res = _dot(sgw, gvn[ci * SGU_CHUNK:(ci + 1) * SGU_CHUNK])
            m = res[0:SGU_CHUNK]
            for h in range(1, HEADS):
                m = jnp.where(lane == h, res[h * SGU_CHUNK:(h + 1) * SGU_CHUNK], m)
            mixed.append(m + sgb_ref[...])
        y_sgu = u * jnp.concatenate(mixed, axis=0)

        y_na = na_ref[rows, :].astype(F32)
        merged = jnp.concatenate([_rms(y_conv[rows]), _rms(y_rwkv), _rms(y_na), _rms(y_sgu)], axis=1)
        return (merged * mg_ref[...]).astype(BF16)

    half = tm // 2
    for r0 in (0, half):
        rows = slice(r0, r0 + half)
        out = jnp.dot(mixers(rows), wout_ref[...], preferred_element_type=F32)
        o_ref[rows, :] = x_ref[rows, :] + _rms(out) * gpost_ref[...]


def _merge(x2, conv, rkv, lora, yf, yb, y_na, sgu, batch, l, seg, convw, gup, rk, lnw, lnb, sgn, sgw, sgb,
           mg, wout, gpost):
    n = x2.shape[0]
    tm = ROW_TILE
    tiles_per_seq = n // batch // tm
    nh = n // 16
    row = lambda wdt: pl.BlockSpec((tm, wdt), lambda i: (i, 0))
    prev = pl.BlockSpec((16, 3 * GROUP_W), lambda i: (jnp.maximum(i * (tm // 16) - 1, 0), 0))
    nxt = pl.BlockSpec((16, 3 * GROUP_W), lambda i: (jnp.minimum((i + 1) * (tm // 16), nh - 1), 0))
    params = (seg, convw, gup, rk, lnw, lnb, sgn, sgw, sgb, mg, wout, gpost)
    return pl.pallas_call(
        functools.partial(_merge_body, tiles_per_seq=tiles_per_seq),
        grid=(n // tm,),
        in_specs=[row(D_MODEL), row(3 * GROUP_W), prev, nxt, row(3 * GROUP_W), row(LORA_W),
                  row(GROUP_W), row(GROUP_W), row(GROUP_W), row(2 * GROUP_W)]
                 + [_full(p.shape) if p is seg else _layer(p, l) for p in params],
        out_specs=row(D_MODEL),
        out_shape=jax.ShapeDtypeStruct((n, D_MODEL), F32),
        compiler_params=pltpu.CompilerParams(dimension_semantics=("parallel",),
                                             vmem_limit_bytes=VMEM_LIMIT),
        name="merge",
    )(x2, conv, conv, conv, rkv, lora, yf, yb, y_na, sgu, *params)


FFN_SLABS = ((0, 768), (768, 768), (1536, 768), (2304, 512))
assert sum(w for _, w in FFN_SLABS) == D_FF
HALO = 8


def _ffn_body(x_ref, xprev_ref, xnext_ref, gpre_ref, wup_ref, cw_ref, wdn_ref, gpost_ref, o_ref, *,
              tiles_per_seq):
    tm = ROW_TILE
    i = pl.program_id(0)
    first = (i % tiles_per_seq) == 0
    last = (i % tiles_per_seq) == tiles_per_seq - 1
    x = x_ref[...]
    g = gpre_ref[...]
    hf = _rms(x) * g
    hp = jnp.where(first, 0.0, _rms(xprev_ref[...]) * g)
    hn = jnp.where(last, 0.0, _rms(xnext_ref[...]) * g)
    h = hf.astype(BF16)
    hext = jnp.concatenate([hp, hf, hn], axis=0).astype(BF16)
    ext = tm + 2 * HALO

    def up(s):
        c0, w = FFN_SLABS[s]
        gate = jnp.dot(hext, wup_ref[:, c0:c0 + w], preferred_element_type=F32)
        lin = jnp.dot(h, wup_ref[:, D_FF + c0:D_FF + c0 + w], preferred_element_type=F32)
        return gate, lin

    def down(s, gate, lin):
        c0, w = FFN_SLABS[s]
        cw = cw_ref[:, c0:c0 + w]
        g_prev = pltpu.roll(gate, 1, axis=0)[HALO:HALO + tm]
        g_next = pltpu.roll(gate, ext - 1, axis=0)[HALO:HALO + tm]
        cv = g_prev * cw[0:1] + gate[HALO:HALO + tm] * cw[1:2] + g_next * cw[2:3]
        hid = jax.nn.gelu(cv) * lin
        return jnp.dot(hid.astype(BF16), wdn_ref[c0:c0 + w, :], preferred_element_type=F32)

    acc = None
    pending = up(0)
    for s in range(len(FFN_SLABS)):
        nxt = up(s + 1) if s + 1 < len(FFN_SLABS) else None
        part = down(s, *pending)
        acc = part if acc is None else acc + part
        pending = nxt
    o_ref[...] = x + _rms(acc) * gpost_ref[...]


def _ffn(x2, batch, l, gpre, wup, cw, wdn, gpost):
    n = x2.shape[0]
    tm = ROW_TILE
    tiles_per_seq = n // batch // tm
    nh = n // HALO
    row = pl.BlockSpec((tm, D_MODEL), lambda i: (i, 0))
    prev = pl.BlockSpec((HALO, D_MODEL), lambda i: (jnp.maximum(i * (tm // HALO) - 1, 0), 0))
    nxt = pl.BlockSpec((HALO, D_MODEL), lambda i: (jnp.minimum((i + 1) * (tm // HALO), nh - 1), 0))
    params = (gpre, wup, cw, wdn, gpost)
    return pl.pallas_call(
        functools.partial(_ffn_body, tiles_per_seq=tiles_per_seq),
        grid=(n // tm,),
        in_specs=[row, prev, nxt] + [_layer(p, l) for p in params],
        out_specs=row,
        out_shape=jax.ShapeDtypeStruct((n, D_MODEL), F32),
        compiler_params=pltpu.CompilerParams(dimension_semantics=("parallel",),
                                             vmem_limit_bytes=VMEM_LIMIT),
        name="convffn",
    )(x2, x2, x2, *params)


def _regroup_w_in(w):
    g = GROUP_W
    lora0 = 6 * g
    lora1 = lora0 + 2 * DECAY_RANK + 2 * ICLR_RANK + GATE_RANK
    w = w.astype(BF16)
    pad = jnp.zeros(w.shape[:-1] + (LORA_W - (lora1 - lora0),), BF16)
    return jnp.concatenate([w[..., :lora0], w[..., lora1:lora1 + 5 * g], w[..., lora0:lora1], pad], axis=-1)


def _lora_up(w, start):
    return jnp.pad(w, ((0, 0), (start, LORA_W - start - w.shape[1]), (0, 0)))


def _all_params(p, rows):
    nl = p["w_in"].shape[0]
    both = lambda w, start: jnp.concatenate(
        [_lora_up(w[:, d], start + d * w.shape[2]) for d in range(2)], axis=-1)
    wd = both(p["rwkv_w_up"], 0)[:, :DECAY_LANES]
    wd_hi = wd.astype(BF16)
    wd_lo = (wd - wd_hi.astype(F32)).astype(BF16)
    wup = jnp.concatenate([wd_hi, wd_lo, wd_hi, jnp.zeros_like(wd_hi)], axis=1)
    row2 = lambda a: a.reshape(nl, 1, -1)
    return dict(
        w_in=_regroup_w_in(p["w_in"]), g_mix_pre=row2(p["norm_mix_pre"]),
        w0=row2(p["rwkv_w0"]), wup=wup, a0=row2(p["rwkv_a0"]),
        aup=both(p["rwkv_a_up"], 2 * DECAY_RANK).astype(BF16),
        k_k=row2(p["rwkv_k_k"]), k_a=row2(p["rwkv_k_a"]),
        bias=_na_bias_table(p["na_rpb"], rows),
        convw=p["conv_a_w"], gup=_lora_up(p["rwkv_g_up"], 2 * DECAY_RANK + 2 * ICLR_RANK).astype(BF16),
        rk=row2(p["rwkv_r_k"]), lnw=row2(p["rwkv_lnx_w"]), lnb=row2(p["rwkv_lnx_b"]), sgn=row2(p["sgu_norm"]),
        sgw=p["sgu_w"].reshape(nl, HEADS * SGU_CHUNK, SGU_CHUNK).astype(BF16),
        sgb=jnp.repeat(jnp.swapaxes(p["sgu_b"], 1, 2), HEAD_DIM, axis=2),
        mg=row2(p["merge_gain"]), wout=p["w_out"].astype(BF16), g_mix_post=row2(p["norm_mix_post"]),
        g_ffn_pre=row2(p["norm_ffn_pre"]), ffn_up=p["ffn_w_up"].astype(BF16), ffn_conv=p["ffn_conv"],
        ffn_down=p["ffn_w_down"].astype(BF16), g_ffn_post=row2(p["norm_ffn_post"]),
    )


def _constants():
    i = jnp.arange(WKV_CHUNK)
    tri = jnp.stack([i[None, :] <= i[:, None], i[None, :] >= i[:, None]]).astype(BF16)
    g = jnp.arange(GROUP_W) // HEAD_DIM
    seg = (g[:, None] == g[None, :]).astype(BF16)
    return tri, seg


def _forward(x, p):
    batch, t, _ = x.shape
    assert t % ROW_TILE == 0 and t % (NA_ROWS * GRID_W) == 0 and t // GRID_W >= NA_KH
    assert (batch * t) % INPROJ_TILE == 0 and t % WKV_BLOCK == 0
    x2 = x.reshape(batch * t, D_MODEL)
    tri, seg = _constants()
    q = _all_params(p, t // GRID_W)
    for l in range(p["w_in"].shape[0]):
        conv, rkvk, na, sgu, lora, lw, nalr = _inproj(x2, l, q["g_mix_pre"], q["w_in"], seg, q["k_k"], q["w0"],
                                                      q["wup"], q["a0"], q["aup"])
        yf, yb = _wkv(rkvk, lw, nalr, batch, l, tri, q["k_a"])
        y_na = _na(na, q["bias"], batch, l)
        x2 = _merge(x2, conv, rkvk, lora, yf, yb, y_na, sgu, batch, l, seg, q["convw"], q["gup"], q["rk"],
                    q["lnw"], q["lnb"], q["sgn"], q["sgw"], q["sgb"], q["mg"], q["wout"], q["g_mix_post"])
        x2 = _ffn(x2, batch, l, q["g_ffn_pre"], q["ffn_up"], q["ffn_conv"], q["ffn_down"], q["g_ffn_post"])
    return x2.reshape(batch, t, D_MODEL)


def kernel(x, norm_mix_pre, norm_mix_post, norm_ffn_pre, norm_ffn_post, w_in, conv_a_w, rwkv_w0, rwkv_w_up,
           rwkv_a0, rwkv_a_up, rwkv_g_up, rwkv_k_k, rwkv_k_a, rwkv_r_k, rwkv_lnx_w, rwkv_lnx_b, na_rpb,
           sgu_norm, sgu_w, sgu_b, merge_gain, w_out, ffn_w_up, ffn_conv, ffn_w_down):
    p = dict(norm_mix_pre=norm_mix_pre, norm_mix_post=norm_mix_post, norm_ffn_pre=norm_ffn_pre,
             norm_ffn_post=norm_ffn_post, w_in=w_in, conv_a_w=conv_a_w, rwkv_w0=rwkv_w0, rwkv_w_up=rwkv_w_up,
             rwkv_a0=rwkv_a0, rwkv_a_up=rwkv_a_up, rwkv_g_up=rwkv_g_up, rwkv_k_k=rwkv_k_k, rwkv_k_a=rwkv_k_a,
             rwkv_r_k=rwkv_r_k, rwkv_lnx_w=rwkv_lnx_w, rwkv_lnx_b=rwkv_lnx_b, na_rpb=na_rpb, sgu_norm=sgu_norm,
             sgu_w=sgu_w, sgu_b=sgu_b, merge_gain=merge_gain, w_out=w_out, ffn_w_up=ffn_w_up, ffn_conv=ffn_conv,
             ffn_w_down=ffn_w_down)
    return _forward(x, p)
```
